```python
import jax, jax.numpy as jnp
from jax import lax
import numpy as np

D_MODEL = 1024
BATCH = 4
SEQ = 4096
DEPTH = 1
DEC_BATCH = 128
DEC_SEQ = 8
PAST_LEN = 16384
PAGE_SIZE = 128

LRU_WIDTH = D_MODEL
LRU_BLOCKS = 16
LRU_BLOCK = LRU_WIDTH // LRU_BLOCKS
CONV_WIDTH = 4
LRU_C = 8.0
N_HEADS = 16
N_KV_HEADS = 4
HEAD_DIM = 64
GROUP = N_HEADS // N_KV_HEADS
Q_WIDTH = N_HEADS * HEAD_DIM
KV_WIDTH = N_KV_HEADS * HEAD_DIM
WINDOW = 128
ATTN_BLOCK = 128
PEER_HEADS = 8
N_KEYS = 128
N_EXPERTS = N_KEYS * N_KEYS
KEY_DIM = 128
PEER_TOPK = 16
N_ACTIVE = PEER_HEADS * PEER_TOPK
PEER_CHUNK = 128
IN_SPLITS = (LRU_WIDTH, LRU_WIDTH, Q_WIDTH, KV_WIDTH, KV_WIDTH, D_MODEL, D_MODEL)
IN_COLS = LRU_WIDTH * 2 + Q_WIDTH + KV_WIDTH * 2 + D_MODEL * 2
EPS = 1e-6
NEG_INF = -1e30

kernel_name = 'griffin_swa_sink_peer_hybrid_step'


def _rmsnorm(x, g):
    xf = x.astype(jnp.float32)
    y = xf * lax.rsqrt(jnp.mean(xf * xf, axis=-1, keepdims=True) + EPS)
    return (y * g.astype(jnp.float32)).astype(x.dtype)


def _split_in(z):
    parts, start = [], 0
    for w in IN_SPLITS:
        parts.append(z[..., start:start + w])
        start += w
    return parts


def _alibi_slopes():
    return 2.0 ** (-8.0 * jnp.arange(1, N_HEADS + 1, dtype=jnp.float32) / N_HEADS)


def _causal_conv(xr, buf, w, b):
    T = xr.shape[1]
    full = jnp.concatenate([buf.astype(xr.dtype), xr], axis=1)
    y = sum(full[:, j:j + T] * w[j] for j in range(CONV_WIDTH)) + b
    return y, full[:, -(CONV_WIDTH - 1):]


def _lin_combine(left, right):
    a1, b1 = left
    a2, b2 = right
    return a1 * a2, a2 * b1 + b2


def _rglru(xc, h0, w_a, b_a, w_x, b_x, lam):
    B, T, C = xc.shape
    xb = xc.reshape(B, T, LRU_BLOCKS, LRU_BLOCK)
    r = jax.nn.sigmoid((jnp.einsum('btnj,njk->btnk', xb, w_a).reshape(B, T, C) + b_a).astype(jnp.float32))
    i = jax.nn.sigmoid((jnp.einsum('btnj,njk->btnk', xb, w_x).reshape(B, T, C) + b_x).astype(jnp.float32))
    log_a = LRU_C * r * jax.nn.log_sigmoid(lam.astype(jnp.float32))
    a = jnp.exp(log_a)
    bterm = jnp.sqrt(-jnp.expm1(2.0 * log_a)) * (i * xc.astype(jnp.float32))
    bterm = bterm.at[:, 0].add(a[:, 0] * h0.astype(jnp.float32))
    _, h = lax.associative_scan(_lin_combine, (a, bterm), axis=1)
    return h, h[:, -1]


def _window_attend(q, k, v, qpos, kpos, sinks):
    B, L, Tq = q.shape[:3]
    s = jnp.einsum('blqkgd,blskd->blkgqs', q, k).astype(jnp.float32) * (HEAD_DIM ** -0.5)
    dist = qpos[:, :, None] - kpos[:, None, :]
    valid = (dist >= 0) & (dist <= WINDOW) & (kpos[:, None, :] >= 0)
    slopes = _alibi_slopes().reshape(N_KV_HEADS, GROUP)
    s = s - slopes[None, None, :, :, None, None] * dist.astype(jnp.float32)[None, :, None, None]
    s = jnp.where(valid[None, :, None, None], s, NEG_INF)
    sink = jnp.broadcast_to(sinks.astype(jnp.float32).reshape(1, 1, N_KV_HEADS, GROUP, 1, 1), s.shape[:-1] + (1,))
    p = jax.nn.softmax(jnp.concatenate([s, sink], axis=-1), axis=-1)[..., :-1]
    o = jnp.einsum('blkgqs,blskd->blqkgd', p.astype(v.dtype), v)
    return o.reshape(B, L * Tq, Q_WIDTH)


def _band(t):
    B, S = t.shape[:2]
    tb = t.reshape(B, S // ATTN_BLOCK, ATTN_BLOCK, N_KV_HEADS, HEAD_DIM)
    prev = jnp.concatenate([jnp.zeros_like(tb[:, :1]), tb[:, :-1]], axis=1)
    return jnp.concatenate([prev, tb], axis=2)


def _peer(xn, w_query, sub_keys, expert_u, expert_v):
    B, T, D = xn.shape
    n = B * T
    xt = xn.reshape(n, D)
    qr = (xt @ w_query).reshape(n, PEER_HEADS, 2, KEY_DIM)
    s = jnp.einsum('nhpd,hpkd->nhpk', qr, sub_keys).astype(jnp.float32)
    s_top, i_top = lax.top_k(s, PEER_TOPK)
    cand = (s_top[:, :, 0, :, None] + s_top[:, :, 1, None, :]).reshape(n, PEER_HEADS, PEER_TOPK * PEER_TOPK)
    cand_idx = (i_top[:, :, 0, :, None] * N_KEYS + i_top[:, :, 1, None, :]).reshape(n, PEER_HEADS, PEER_TOPK * PEER_TOPK)
    best, pos = lax.top_k(cand, PEER_TOPK)
    idx = jnp.take_along_axis(cand_idx, pos, axis=-1).reshape(n, N_ACTIVE)
    g = jax.nn.softmax(best, axis=-1).reshape(n, N_ACTIVE).astype(xt.dtype)
    pad = (-n) % PEER_CHUNK
    xt_p = jnp.pad(xt, ((0, pad), (0, 0))).reshape(-1, PEER_CHUNK, D)
    idx_p = jnp.pad(idx, ((0, pad), (0, 0))).reshape(-1, PEER_CHUNK, N_ACTIVE)
    g_p = jnp.pad(g, ((0, pad), (0, 0))).reshape(-1, PEER_CHUNK, N_ACTIVE)

    def expert_block(args):
        xc, ic, gc = args
        act = jax.nn.gelu(jnp.einsum('cd,ced->ce', xc, expert_u[ic])) * gc
        return jnp.einsum('ce,ced->cd', act, expert_v[ic])

    out = lax.map(expert_block, (xt_p, idx_p, g_p)).reshape(-1, D)[:n]
    return out.reshape(B, T, D)


def _layer(x, conv_buf, h0, k_past, v_past, norm1_g, w_in, conv_w, conv_b, rg_w_a, rg_b_a, rg_w_x, rg_b_x,
           rg_lambda, q_norm_g, k_norm_g, attn_sinks, w_branch_lru, w_branch_attn, w_out, norm2_g,
           peer_w_query, peer_sub_keys, expert_u, expert_v):
    B, T, _ = x.shape
    xn = _rmsnorm(x, norm1_g)
    xr, gr, q, k, v, ga, gb = _split_in(xn @ w_in)
    xc, new_conv = _causal_conv(xr, conv_buf, conv_w, conv_b)
    h, h_last = _rglru(xc, h0, rg_w_a, rg_b_a, rg_w_x, rg_b_x, rg_lambda)
    lru_out = h.astype(x.dtype) * jax.nn.gelu(gr)
    q = _rmsnorm(q.reshape(B, T, N_KV_HEADS, GROUP, HEAD_DIM), q_norm_g)
    k = _rmsnorm(k.reshape(B, T, N_KV_HEADS, HEAD_DIM), k_norm_g)
    v = v.reshape(B, T, N_KV_HEADS, HEAD_DIM)
    if k_past is None:
        nb = T // ATTN_BLOCK
        start = jnp.arange(nb)[:, None] * ATTN_BLOCK
        qpos = start + jnp.arange(ATTN_BLOCK)[None]
        kpos = start - ATTN_BLOCK + jnp.arange(2 * ATTN_BLOCK)[None]
        qb = q.reshape(B, nb, ATTN_BLOCK, N_KV_HEADS, GROUP, HEAD_DIM)
        attn = _window_attend(qb, _band(k), _band(v), qpos, kpos, attn_sinks)
        new_k, new_v = k[:, -WINDOW:], v[:, -WINDOW:]
    else:
        kk = jnp.concatenate([k_past.astype(k.dtype), k], axis=1)
        vv = jnp.concatenate([v_past.astype(v.dtype), v], axis=1)
        qpos = (PAST_LEN + jnp.arange(T))[None]
        kpos = (PAST_LEN - WINDOW + jnp.arange(WINDOW + T))[None]
        attn = _window_attend(q[:, None], kk[:, None], vv[:, None], qpos, kpos, attn_sinks)
        new_k, new_v = kk[:, -WINDOW:], vv[:, -WINDOW:]
    merged = jax.nn.sigmoid(ga) * (lru_out @ w_branch_lru) + jax.nn.sigmoid(gb) * (attn @ w_branch_attn)
    hres = x + merged @ w_out
    y = hres + _peer(_rmsnorm(hres, norm2_g), peer_w_query, peer_sub_keys, expert_u, expert_v)
    return y, new_conv, h_last, new_k, new_v


def setup_inputs(seed: int = 0) -> dict:
    key = jax.random.key(seed)
    ks = jax.random.split(key, 32)
    f32 = jnp.float32
    nrm = lambda k, shape, scale: jax.random.normal(k, shape, f32) * scale
    u = jax.random.uniform(ks[14], (DEPTH, LRU_WIDTH), f32, minval=0.9, maxval=0.999)
    sg = u ** (1.0 / LRU_C)
    lam = jnp.log(sg) - jnp.log1p(-sg)
    return {
        'x_prompt': nrm(ks[0], (BATCH, SEQ, D_MODEL), 1.0),
        'x_sample': nrm(ks[1], (DEC_BATCH, DEC_SEQ, D_MODEL), 1.0),
        'cache_conv': nrm(ks[2], (DEPTH, DEC_BATCH, CONV_WIDTH - 1, LRU_WIDTH), 1.0),
        'state_lru': nrm(ks[3], (DEPTH, DEC_BATCH, LRU_WIDTH), 0.5),
        'cache_k': nrm(ks[4], (DEPTH, DEC_BATCH, WINDOW, N_KV_HEADS, HEAD_DIM), 1.0),
        'cache_v': nrm(ks[5], (DEPTH, DEC_BATCH, WINDOW, N_KV_HEADS, HEAD_DIM), 1.0),
        'norm1_g': 1.0 + nrm(ks[6], (DEPTH, D_MODEL), 0.02),
        'w_in': nrm(ks[7], (DEPTH, D_MODEL, IN_COLS), D_MODEL ** -0.5),
        'conv_w': nrm(ks[8], (DEPTH, CONV_WIDTH, LRU_WIDTH), CONV_WIDTH ** -0.5),
        'conv_b': nrm(ks[9], (DEPTH, LRU_WIDTH), 0.01),
        'rg_w_a': nrm(ks[10], (DEPTH, LRU_BLOCKS, LRU_BLOCK, LRU_BLOCK), LRU_BLOCK ** -0.5),
        'rg_b_a': nrm(ks[11], (DEPTH, LRU_WIDTH), 0.01),
        'rg_w_x': nrm(ks[12], (DEPTH, LRU_BLOCKS, LRU_BLOCK, LRU_BLOCK), LRU_BLOCK ** -0.5),
        'rg_b_x': nrm(ks[13], (DEPTH, LRU_WIDTH), 0.01),
        'rg_lambda': lam,
        'q_norm_g': 1.0 + nrm(ks[15], (DEPTH, HEAD_DIM), 0.02),
        'k_norm_g': 1.0 + nrm(ks[16], (DEPTH, HEAD_DIM), 0.02),
        'attn_sinks': nrm(ks[17], (DEPTH, N_HEADS), 0.5),
        'w_branch_lru': nrm(ks[18], (DEPTH, LRU_WIDTH, D_MODEL), LRU_WIDTH ** -0.5),
        'w_branch_attn': nrm(ks[19], (DEPTH, Q_WIDTH, D_MODEL), Q_WIDTH ** -0.5),
        'w_out': nrm(ks[20], (DEPTH, D_MODEL, D_MODEL), D_MODEL ** -0.5),
        'norm2_g': 1.0 + nrm(ks[21], (DEPTH, D_MODEL), 0.02),
        'peer_w_query': nrm(ks[22], (DEPTH, D_MODEL, PEER_HEADS * 2 * KEY_DIM), D_MODEL ** -0.5),
        'peer_sub_keys': nrm(ks[23], (DEPTH, PEER_HEADS, 2, N_KEYS, KEY_DIM), KEY_DIM ** -0.5),
        'expert_u': nrm(ks[24], (DEPTH, N_EXPERTS, D_MODEL), D_MODEL ** -0.5),
        'expert_v': nrm(ks[25], (DEPTH, N_EXPERTS, D_MODEL), 0.1),
    }


def reference(x_prompt, x_sample, cache_conv, state_lru, cache_k, cache_v, norm1_g, w_in, conv_w, conv_b,
              rg_w_a, rg_b_a, rg_w_x, rg_b_x, rg_lambda, q_norm_g, k_norm_g, attn_sinks, w_branch_lru,
              w_branch_attn, w_out, norm2_g, peer_w_query, peer_sub_keys, expert_u, expert_v):
    yp, ys = x_prompt, x_sample
    conv_p, lru_p, k_p, v_p = [], [], [], []
    conv_s, lru_s, k_s, v_s = [], [], [], []
    for l in range(DEPTH):
        lp = (norm1_g[l], w_in[l], conv_w[l], conv_b[l], rg_w_a[l], rg_b_a[l], rg_w_x[l], rg_b_x[l],
              rg_lambda[l], q_norm_g[l], k_norm_g[l], attn_sinks[l], w_branch_lru[l], w_branch_attn[l],
              w_out[l], norm2_g[l], peer_w_query[l], peer_sub_keys[l], expert_u[l], expert_v[l])
        zero_buf = jnp.zeros((yp.shape[0], CONV_WIDTH - 1, LRU_WIDTH), yp.dtype)
        zero_h = jnp.zeros((yp.shape[0], LRU_WIDTH), jnp.float32)
        yp, c, h, kw, vw = _layer(yp, zero_buf, zero_h, None, None, *lp)
        conv_p.append(c); lru_p.append(h); k_p.append(kw); v_p.append(vw)
        ys, c, h, kw, vw = _layer(ys, cache_conv[l], state_lru[l], cache_k[l], cache_v[l], *lp)
        conv_s.append(c); lru_s.append(h); k_s.append(kw); v_s.append(vw)
    new_conv_p, new_lru_p = jnp.stack(conv_p), jnp.stack(lru_p)
    new_k_p, new_v_p = jnp.stack(k_p), jnp.stack(v_p)
    new_conv_s, new_lru_s = jnp.stack(conv_s), jnp.stack(lru_s)
    new_k_s, new_v_s = jnp.stack(k_s), jnp.stack(v_s)
    return (yp, ys, new_conv_p, new_lru_p, new_k_p, new_v_p, new_conv_s, new_lru_s, new_k_s, new_v_s)
```

```python
import functools

import jax
import jax.numpy as jnp
from jax import lax
from jax.experimental import pallas as pl
from jax.experimental.pallas import tpu as pltpu

D_MODEL = 1024
LRU_WIDTH = 1024
LRU_BLOCKS = 16
LRU_BLOCK = LRU_WIDTH // LRU_BLOCKS
CONV_WIDTH = 4
LRU_C = 8.0
N_HEADS = 16
N_KV_HEADS = 4
HEAD_DIM = 64
GROUP = N_HEADS // N_KV_HEADS
Q_WIDTH = N_HEADS * HEAD_DIM
KV_WIDTH = N_KV_HEADS * HEAD_DIM
WINDOW = 128
ATTN_BLOCK = 128
PAST_LEN = 16384
PEER_HEADS = 8
N_KEYS = 128
N_EXPERTS = N_KEYS * N_KEYS
KEY_DIM = 128
PEER_TOPK = 16
N_ACTIVE = PEER_HEADS * PEER_TOPK
IN_SPLITS = (LRU_WIDTH, LRU_WIDTH, Q_WIDTH, KV_WIDTH, KV_WIDTH, D_MODEL, D_MODEL)
IN_COLS = sum(IN_SPLITS)
EPS = 1e-6
NEG_INF = -1e30

LANES = 128
SUBLANES = 8
ROW_CHUNKS = D_MODEL // LANES
VMEM_LIMIT = 56 * 1024 * 1024

F32 = jnp.float32
BF16 = jnp.bfloat16


def _cparams(sem, vmem=None):
    return pltpu.CompilerParams(dimension_semantics=sem, vmem_limit_bytes=vmem)


def _dot(a, b):
    return jnp.dot(a, b, preferred_element_type=F32)


def _dot_nt(a, b):
    return lax.dot_general(a, b, (((1,), (1,)), ((), ())), preferred_element_type=F32)


def _dot_exact01(x, m01):
    hi = x.astype(BF16)
    r1 = x - hi.astype(F32)
    mid = r1.astype(BF16)
    lo = (r1 - mid.astype(F32)).astype(BF16)
    return _dot(hi, m01) + _dot(mid, m01) + _dot(lo, m01)


def _seg_matrix(width, seg, cols):
    c = lax.broadcasted_iota(jnp.int32, (width, cols), 0)
    h = lax.broadcasted_iota(jnp.int32, (width, cols), 1)
    return jnp.where(c // seg == h, 1.0, 0.0).astype(BF16)


def _seg_matrix_t(cols, width, seg):
    h = lax.broadcasted_iota(jnp.int32, (cols, width), 0)
    c = lax.broadcasted_iota(jnp.int32, (cols, width), 1)
    return jnp.where(c // seg == h, 1.0, 0.0).astype(BF16)


def _head_rmsnorm(t, gain_row, width):
    seg = _seg_matrix(width, HEAD_DIM, LANES)
    seg_t = _seg_matrix_t(LANES, width, HEAD_DIM)
    ssq = _dot_exact01(t * t, seg)
    inv = lax.rsqrt(ssq * (1.0 / HEAD_DIM) + EPS)
    inv_b = _dot_exact01(inv, seg_t)
    return t * inv_b * gain_row


def _inproj_kernel(x_ref, g1_ref, w_ref, qg_ref, kg_ref,
                   xr_ref, ggr_ref, q_ref, k_ref, v_ref, sga_ref, sgb_ref):
    x = x_ref[...]
    y = x * lax.rsqrt(jnp.mean(x * x, axis=-1, keepdims=True) + EPS)
    xn = (y * g1_ref[...]).astype(BF16)
    o = 0
    xr_ref[...] = _dot(xn, w_ref[:, o:o + LRU_WIDTH]); o += LRU_WIDTH
    ggr_ref[...] = jax.nn.gelu(_dot(xn, w_ref[:, o:o + LRU_WIDTH])); o += LRU_WIDTH
    q = _dot(xn, w_ref[:, o:o + Q_WIDTH]); o += Q_WIDTH
    q_ref[...] = (_head_rmsnorm(q, qg_ref[...], Q_WIDTH) * (HEAD_DIM ** -0.5)).astype(BF16)
    k = _dot(xn, w_ref[:, o:o + KV_WIDTH]); o += KV_WIDTH
    k_ref[...] = _head_rmsnorm(k, kg_ref[...], KV_WIDTH)
    v_ref[...] = _dot(xn, w_ref[:, o:o + KV_WIDTH]); o += KV_WIDTH
    sga_ref[...] = jax.nn.sigmoid(_dot(xn, w_ref[:, o:o + D_MODEL])); o += D_MODEL
    sgb_ref[...] = jax.nn.sigmoid(_dot(xn, w_ref[:, o:o + D_MODEL]))


def _inproj(x, norm1_g, w_in_bf16, q_norm_g, k_norm_g, tm=256):
    n = x.shape[0]
    assert n % tm == 0
    row = lambda w: pl.BlockSpec((tm, w), lambda i: (i, 0))
    full = lambda a: pl.BlockSpec(a.shape, lambda i: (0,) * a.ndim)
    g1 = norm1_g.reshape(1, D_MODEL)
    qg = jnp.tile(q_norm_g, N_HEADS).reshape(1, Q_WIDTH)
    kg = jnp.tile(k_norm_g, N_KV_HEADS).reshape(1, KV_WIDTH)
    outs = [(LRU_WIDTH, F32), (LRU_WIDTH, F32), (Q_WIDTH, BF16), (KV_WIDTH, F32), (KV_WIDTH, F32),
            (D_MODEL, F32), (D_MODEL, F32)]
    return pl.pallas_call(
        _inproj_kernel,
        grid=(n // tm,),
        in_specs=[row(D_MODEL), full(g1), full(w_in_bf16), full(qg), full(kg)],
        out_specs=[row(w) for w, _ in outs],
        out_shape=[jax.ShapeDtypeStruct((n, w), dt) for w, dt in outs],
        compiler_params=_cparams(("parallel",), VMEM_LIMIT),
        name="inproj",
    )(x, g1, w_in_bf16, qg, kg)


def _log_sigmoid(x):
    return jnp.minimum(x, 0.0) - jnp.log1p(jnp.exp(-jnp.abs(x)))


def _lru_gates(xc, wg_ref, ba_ref, bx_ref, lam_ref):
    g = _dot(xc.astype(BF16), wg_ref[...])
    r = jax.nn.sigmoid(g[:, :LRU_WIDTH] + ba_ref[...])
    i = jax.nn.sigmoid(g[:, LRU_WIDTH:] + bx_ref[...])
    log_a = LRU_C * r * _log_sigmoid(lam_ref[...])
    a = jnp.exp(log_a)
    b = jnp.sqrt(-jnp.tanh(log_a) * (a * a + 1.0)) * (i * xc)
    return a, b


def _shift_rows(x, s, tpos, fill):
    return jnp.where(tpos < s, fill, pltpu.roll(x, s, axis=0))


def _segment_scan(a, b, tpos, seg_len):
    s = 1
    while s < seg_len:
        a_prev = _shift_rows(a, s, tpos, 1.0)
        b_prev = _shift_rows(b, s, tpos, 0.0)
        b = a * b_prev + b
        a = a * a_prev
        s *= 2
    return a, b


def _lru_prompt_kernel(xr_ref, ggr_ref, cw_ref, cb_ref, wg_ref, ba_ref, bx_ref, lam_ref,
                       out_ref, hlast_ref, prev_ref, h_ref):
    tc = xr_ref.shape[0]

    @pl.when(pl.program_id(1) == 0)
    def _():
        prev_ref[...] = jnp.zeros_like(prev_ref)
        h_ref[...] = jnp.zeros_like(h_ref)

    xr = xr_ref[...]
    tpos = lax.broadcasted_iota(jnp.int32, (tc, LRU_WIDTH), 0)
    tpos8 = lax.broadcasted_iota(jnp.int32, (SUBLANES, LRU_WIDTH), 0)
    prev = prev_ref[...]
    xc = xr * cw_ref[CONV_WIDTH - 1:CONV_WIDTH, :] + cb_ref[...]
    for s in range(1, CONV_WIDTH):
        rolled = pltpu.roll(xr, s, axis=0)
        head = jnp.where(tpos8 < s, pltpu.roll(prev, s, axis=0), rolled[:SUBLANES])
        shifted = jnp.concatenate([head, rolled[SUBLANES:]], axis=0)
        xc = xc + shifted * cw_ref[CONV_WIDTH - 1 - s:CONV_WIDTH - s, :]
    a, b = _lru_gates(xc, wg_ref, ba_ref, bx_ref, lam_ref)
    a_cum, b_cum = _segment_scan(a, b, tpos, tc)
    h = a_cum * h_ref[0:1, :] + b_cum
    out_ref[...] = (h * ggr_ref[...]).astype(out_ref.dtype)
    last8 = h[tc - SUBLANES:]
    hlast_ref[...] = last8
    h_ref[...] = jnp.broadcast_to(last8[SUBLANES - 1:], h_ref.shape)
    prev_ref[...] = xr[tc - SUBLANES:]


def _lru_sample_kernel(xr_ref, ggr_ref, halo_ref, h0_ref, cw_ref, cb_ref, wg_ref, ba_ref, bx_ref,
                       lam_ref, out_ref, h_out_ref):
    rows = xr_ref.shape[0]
    xr = xr_ref[...]
    tpos = lax.broadcasted_iota(jnp.int32, (rows, LRU_WIDTH), 0) % SUBLANES
    xc = xr * cw_ref[CONV_WIDTH - 1:CONV_WIDTH, :] + cb_ref[...]
    for s in range(1, CONV_WIDTH):
        shifted = _shift_rows(xr, s, tpos, halo_ref[s - 1])
        xc = xc + shifted * cw_ref[CONV_WIDTH - 1 - s:CONV_WIDTH - s, :]
    a, b = _lru_gates(xc, wg_ref, ba_ref, bx_ref, lam_ref)
    b = b + a * h0_ref[...]
    _, h = _segment_scan(a, b, tpos, SUBLANES)
    out_ref[...] = (h * ggr_ref[...]).astype(out_ref.dtype)
    h_out_ref[...] = h


def _lru_weights(conv_w, conv_b, rg_w_a, rg_b_a, rg_w_x, rg_b_x, rg_lambda):
    eye = jnp.eye(LRU_BLOCKS, dtype=F32)
    bd = lambda w: jnp.einsum("njk,nm->njmk", w, eye).reshape(LRU_WIDTH, LRU_WIDTH)
    wg = jnp.concatenate([bd(rg_w_a), bd(rg_w_x)], axis=1).astype(BF16)
    r = lambda v: v.reshape(1, LRU_WIDTH)
    return conv_w, r(conv_b), wg, r(rg_b_a), r(rg_b_x), r(rg_lambda)


def _lru_prompt(xr, ggr, lw, batch, seq, tc=256):
    n = xr.shape[0]
    nt = seq // tc
    row = pl.BlockSpec((tc, LRU_WIDTH), lambda b, t: (b * nt + t, 0))
    full = lambda a: pl.BlockSpec(a.shape, lambda b, t: (0,) * a.ndim)
    out, hlast = pl.pallas_call(
        _lru_prompt_kernel,
        grid=(batch, nt),
        in_specs=[row, row] + [full(a) for a in lw],
        out_specs=[row, pl.BlockSpec((SUBLANES, LRU_WIDTH), lambda b, t: (b, 0))],
        out_shape=[jax.ShapeDtypeStruct((n, LRU_WIDTH), BF16),
                   jax.ShapeDtypeStruct((batch * SUBLANES, LRU_WIDTH), F32)],
        scratch_shapes=[pltpu.VMEM((SUBLANES, LRU_WIDTH), F32), pltpu.VMEM((SUBLANES, LRU_WIDTH), F32)],
        compiler_params=_cparams(("parallel", "arbitrary"), VMEM_LIMIT),
        name="lru_prompt",
    )(xr, ggr, *lw)
    return out, hlast.reshape(batch, SUBLANES, LRU_WIDTH)[:, SUBLANES - 1]


def _lru_sample(xr, ggr, conv_buf, h0, lw, rows_per_step=256):
    n = xr.shape[0]
    nb = n // SUBLANES
    pad = jnp.zeros((nb, SUBLANES - (CONV_WIDTH - 1), LRU_WIDTH), F32)
    halos = []
    for s in range(1, CONV_WIDTH):
        rows = conv_buf[:, CONV_WIDTH - 1 - s:, :]
        halos.append(jnp.concatenate(
            [rows, jnp.zeros((nb, SUBLANES - s, LRU_WIDTH), F32)], axis=1).reshape(n, LRU_WIDTH))
    del pad
    halo = jnp.stack(halos)
    h0_rows = jnp.concatenate(
        [h0[:, None, :], jnp.zeros((nb, SUBLANES - 1, LRU_WIDTH), F32)], axis=1).reshape(n, LRU_WIDTH)
    r = rows_per_step
    row = pl.BlockSpec((r, LRU_WIDTH), lambda i: (i, 0))
    full = lambda a: pl.BlockSpec(a.shape, lambda i: (0,) * a.ndim)
    out, h = pl.pallas_call(
        _lru_sample_kernel,
        grid=(n // r,),
        in_specs=[row, row, pl.BlockSpec((CONV_WIDTH - 1, r, LRU_WIDTH), lambda i: (0, i, 0)), row]
        + [full(a) for a in lw],
        out_specs=[row, row],
        out_shape=[jax.ShapeDtypeStruct((n, LRU_WIDTH), BF16), jax.ShapeDtypeStruct((n, LRU_WIDTH), F32)],
        compiler_params=_cparams(("parallel",), VMEM_LIMIT),
        name="lru_sample",
    )(xr, ggr, halo, h0_rows, *lw)
    return out, h.reshape(nb, SUBLANES, LRU_WIDTH)[:, SUBLANES - 1]


def _softmax_sink_pv(s, sink, v):
    m = jnp.maximum(jnp.max(s, axis=-1, keepdims=True), sink)
    p = jnp.exp(s - m)
    denom = jnp.sum(p, axis=-1, keepdims=True) + jnp.exp(sink - m)
    return _dot(p.astype(BF16), v) / denom


def _attn_prompt_kernel(slopes_ref, sinks_ref, q_ref, kp_ref, kc_ref, vp_ref, vc_ref, o_ref):
    blk = pl.program_id(1)
    tq = ATTN_BLOCK
    qi = lax.broadcasted_iota(jnp.int32, (tq, 2 * tq), 0)
    kj = lax.broadcasted_iota(jnp.int32, (tq, 2 * tq), 1)
    dist = (tq + qi) - kj
    valid = (dist >= 0) & (dist <= WINDOW) & ((kj >= tq) | (blk > 0))
    distf = dist.astype(F32)
    k = jnp.concatenate([kp_ref[...], kc_ref[...]], axis=0).astype(BF16)
    v = jnp.concatenate([vp_ref[...], vc_ref[...]], axis=0).astype(BF16)
    for h in range(N_HEADS):
        g = h // GROUP
        qh = q_ref[:, h * HEAD_DIM:(h + 1) * HEAD_DIM]
        kg = k[:, g * HEAD_DIM:(g + 1) * HEAD_DIM]
        vg = v[:, g * HEAD_DIM:(g + 1) * HEAD_DIM]
        s = _dot_nt(qh, kg) - slopes_ref[h] * distf
        s = jnp.where(valid, s, NEG_INF)
        o = _softmax_sink_pv(s, sinks_ref[h], vg)
        o_ref[:, h * HEAD_DIM:(h + 1) * HEAD_DIM] = o.astype(o_ref.dtype)


def _alibi_slopes():
    return 2.0 ** (-8.0 * jnp.arange(1, N_HEADS + 1, dtype=F32) / N_HEADS)


def _attn_prompt(q, k, v, sinks, batch, seq):
    n = q.shape[0]
    nb = seq // ATTN_BLOCK
    smem = pl.BlockSpec(memory_space=pltpu.SMEM)
    cur = lambda w: pl.BlockSpec((ATTN_BLOCK, w), lambda b, i: (b * nb + i, 0))
    prv = lambda w: pl.BlockSpec((ATTN_BLOCK, w), lambda b, i: (b * nb + jnp.maximum(i - 1, 0), 0))
    return pl.pallas_call(
        _attn_prompt_kernel,
        grid=(batch, nb),
        in_specs=[smem, smem, cur(Q_WIDTH), prv(KV_WIDTH), cur(KV_WIDTH), prv(KV_WIDTH), cur(KV_WIDTH)],
        out_specs=cur(Q_WIDTH),
        out_shape=jax.ShapeDtypeStruct((n, Q_WIDTH), BF16),
        compiler_params=_cparams(("parallel", "arbitrary"), VMEM_LIMIT),
        name="attn_prompt",
    )(_alibi_slopes(), sinks.astype(F32), q, k, k, v, v)


def _attn_sample_kernel(slopes_ref, sinks_ref, q_ref, kn_ref, vn_ref, ck_ref, cv_ref,
                        o_ref, ko_ref, vo_ref):
    nseq = ck_ref.shape[0]
    t = SUBLANES
    keys = WINDOW + t
    qi = lax.broadcasted_iota(jnp.int32, (t, keys), 0)
    kj = lax.broadcasted_iota(jnp.int32, (t, keys), 1)
    dist = (WINDOW + qi) - kj
    valid = (dist >= 0) & (dist <= WINDOW)
    distf = dist.astype(F32)

    def one_seq(b, carry):
        kn = kn_ref[b]
        vn = vn_ref[b]
        ck = ck_ref[b]
        cv = cv_ref[b]
        kall = jnp.concatenate([ck, kn], axis=0)
        vall = jnp.concatenate([cv, vn], axis=0)
        ko_ref[b] = kall[t:]
        vo_ref[b] = vall[t:]
        kb = kall.astype(BF16)
        vb = vall.astype(BF16)
        q = q_ref[b]
        outs = []
        for h in range(N_HEADS):
            g = h // GROUP
            qh = q[:, h * HEAD_DIM:(h + 1) * HEAD_DIM]
            s = _dot_nt(qh, kb[:, g * HEAD_DIM:(g + 1) * HEAD_DIM]) - slopes_ref[h] * distf
            s = jnp.where(valid, s, NEG_INF)
            outs.append(_softmax_sink_pv(s, sinks_ref[h], vb[:, g * HEAD_DIM:(g + 1) * HEAD_DIM]))
        o_ref[b] = jnp.concatenate(outs, axis=-1).astype(o_ref.dtype)
        return carry

    lax.fori_loop(0, nseq, one_seq, 0)


def _attn_sample(q, k_new, v_new, cache_k, cache_v, sinks, seqs_per_step=8):
    nb = cache_k.shape[0]
    t = SUBLANES
    sb = seqs_per_step
    smem = pl.BlockSpec(memory_space=pltpu.SMEM)
    blk = lambda r, w: pl.BlockSpec((sb, r, w), lambda i: (i, 0, 0))
    ck = cache_k.reshape(nb, WINDOW, KV_WIDTH)
    cv = cache_v.reshape(nb, WINDOW, KV_WIDTH)
    o, ko, vo = pl.pallas_call(
        _attn_sample_kernel,
        grid=(nb // sb,),
        in_specs=[smem, smem, blk(t, Q_WIDTH), blk(t, KV_WIDTH), blk(t, KV_WIDTH),
                  blk(WINDOW, KV_WIDTH), blk(WINDOW, KV_WIDTH)],
        out_specs=[blk(t, Q_WIDTH), blk(WINDOW, KV_WIDTH), blk(WINDOW, KV_WIDTH)],
        out_shape=[jax.ShapeDtypeStruct((nb, t, Q_WIDTH), BF16),
                   jax.ShapeDtypeStruct((nb, WINDOW, KV_WIDTH), F32),
                   jax.ShapeDtypeStruct((nb, WINDOW, KV_WIDTH), F32)],
        compiler_params=_cparams(("parallel",), VMEM_LIMIT),
        name="attn_sample",
    )(_alibi_slopes(), sinks.astype(F32), q.reshape(nb, t, Q_WIDTH), k_new.reshape(nb, t, KV_WIDTH),
      v_new.reshape(nb, t, KV_WIDTH), ck, cv)
    shape5 = (nb, WINDOW, N_KV_HEADS, HEAD_DIM)
    return o.reshape(nb * t, Q_WIDTH), ko.reshape(shape5), vo.reshape(shape5)


def _post_kernel(x_ref, lru_ref, attn_ref, sga_ref, sgb_ref, wl_ref, wa_ref, wo_ref, g2_ref,
                 hres_ref, xn2_ref):
    merged = sga_ref[...] * _dot(lru_ref[...], wl_ref[...]) + sgb_ref[...] * _dot(attn_ref[...], wa_ref[...])
    hres = x_ref[...] + _dot(merged.astype(BF16), wo_ref[...])
    hres_ref[...] = hres
    y = hres * lax.rsqrt(jnp.mean(hres * hres, axis=-1, keepdims=True) + EPS)
    xn2_ref[...] = y * g2_ref[...]


def _post(x, lru_out, attn, sga, sgb, wl, wa, wo, norm2_g, tm=256):
    n = x.shape[0]
    row = pl.BlockSpec((tm, D_MODEL), lambda i: (i, 0))
    full = lambda a: pl.BlockSpec(a.shape, lambda i: (0,) * a.ndim)
    g2 = norm2_g.reshape(1, D_MODEL)
    return pl.pallas_call(
        _post_kernel,
        grid=(n // tm,),
        in_specs=[row] * 5 + [full(wl), full(wa), full(wo), full(g2)],
        out_specs=[row, row],
        out_shape=[jax.ShapeDtypeStruct((n, D_MODEL), F32)] * 2,
        compiler_params=_cparams(("parallel",), VMEM_LIMIT),
        name="post",
    )(x, lru_out, attn, sga, sgb, wl, wa, wo, g2)


def _topk_rows(s, k):
    rows = s.shape[0]
    iota = lax.broadcasted_iota(jnp.int32, s.shape, 0)
    vals, idxs = [], []
    for _ in range(k):
        m = jnp.max(s, axis=0, keepdims=True)
        sel = jnp.min(jnp.where(s == m, iota, rows), axis=0, keepdims=True)
        vals.append(m)
        idxs.append(sel)
        s = jnp.where(iota == sel, -jnp.inf, s)
    return jnp.concatenate(vals, axis=0), jnp.concatenate(idxs, axis=0)


def _pick_rows(table, which):
    out = jnp.zeros(which.shape, table.dtype)
    for a in range(table.shape[0]):
        out = out + jnp.where(which == a, table[a:a + 1, :], 0)
    return out


def _route_kernel(xn_ref, wq_ref, keys_ref, idx_ref, gate_ref):
    qr = _dot(xn_ref[...].astype(BF16), wq_ref[...]).astype(BF16)
    for h in range(PEER_HEADS):
        tops = []
        for p in range(2):
            c = h * 2 + p
            st = _dot_nt(keys_ref[c], qr[:, c * KEY_DIM:(c + 1) * KEY_DIM])
            tops.append(_topk_rows(st, PEER_TOPK))
        (s1, i1), (s2, i2) = tops
        cand = jnp.concatenate([s1[a:a + 1, :] + s2 for a in range(PEER_TOPK)], axis=0)
        best, pos = _topk_rows(cand, PEER_TOPK)
        e1 = _pick_rows(i1, pos // PEER_TOPK)
        e2 = _pick_rows(i2, pos % PEER_TOPK)
        idx_ref[h * PEER_TOPK:(h + 1) * PEER_TOPK, :] = e1 * N_KEYS + e2
        ex = jnp.exp(best - best[0:1, :])
        gate_ref[h * PEER_TOPK:(h + 1) * PEER_TOPK, :] = ex / jnp.sum(ex, axis=0, keepdims=True)


def _route(xn2, wq_bf16, keys_bf16, tm=128):
    n = xn2.shape[0]
    full = lambda a: pl.BlockSpec(a.shape, lambda i: (0,) * a.ndim)
    col = pl.BlockSpec((N_ACTIVE, tm), lambda i: (0, i))
    return pl.pallas_call(
        _route_kernel,
        grid=(n // tm,),
        in_specs=[pl.BlockSpec((tm, D_MODEL), lambda i: (i, 0)), full(wq_bf16), full(keys_bf16)],
        out_specs=[col, col],
        out_shape=[jax.ShapeDtypeStruct((N_ACTIVE, n), jnp.int32), jax.ShapeDtypeStruct((N_ACTIVE, n), F32)],
        compiler_params=_cparams(("parallel",), VMEM_LIMIT),
        name="route",
    )(xn2, wq_bf16, keys_bf16)


PACK_ROWS = ROW_CHUNKS // 2
TILE_WORD_ROWS = N_ACTIVE * PACK_ROWS
TILE_ROWS = N_ACTIVE * ROW_CHUNKS


def _pack_table(w):
    wb = w.astype(BF16).reshape(N_EXPERTS, PACK_ROWS, 2, LANES)
    wb = jnp.transpose(wb, (0, 1, 3, 2))
    return lax.bitcast_convert_type(wb, jnp.int32).reshape(N_EXPERTS * PACK_ROWS, LANES)


def _load_table_once(tab_hbm, tab_vmem, sem):
    @pl.when(pl.program_id(0) == 0)
    def _():
        cp = pltpu.make_async_copy(tab_hbm, tab_vmem, sem)
        cp.start()
        cp.wait()


def _gather_tile(idx_ref, t, tab_vmem, tile_ref):
    for j in range(N_ACTIVE):
        e4 = pl.multiple_of(idx_ref[t, j], PACK_ROWS)
        tile_ref[j * PACK_ROWS:(j + 1) * PACK_ROWS, :] = tab_vmem[pl.ds(e4, PACK_ROWS), :]


def _chunk_diag_mask():
    sub = lax.broadcasted_iota(jnp.int32, (ROW_CHUNKS, TILE_ROWS), 0)
    lane = lax.broadcasted_iota(jnp.int32, (ROW_CHUNKS, TILE_ROWS), 1)
    return (lane % ROW_CHUNKS) == sub


def _peer_u_kernel(idx_ref, xn_ref, gate_ref, tab_hbm, sc_ref, tab_vmem, tile_ref, part_ref, sem):
    tb = xn_ref.shape[0]
    _load_table_once(tab_hbm, tab_vmem, sem)
    mask = _chunk_diag_mask()

    def one_token(t, carry):
        _gather_tile(idx_ref, t, tab_vmem, tile_ref)
        rows = pltpu.bitcast(tile_ref[...], BF16)
        xw = xn_ref[t].astype(BF16)
        r = _dot_nt(xw, rows)
        part_ref[t] = jnp.where(mask, r, 0.0)
        return carry

    lax.fori_loop(0, tb, one_token, 0)
    part = part_ref[...].reshape(tb * ROW_CHUNKS, TILE_ROWS)
    per_chunk = _dot_exact01(part, _seg_matrix(TILE_ROWS, ROW_CHUNKS, N_ACTIVE))
    act = _dot_exact01_left(_seg_matrix_t(tb, tb * ROW_CHUNKS, ROW_CHUNKS), per_chunk)
    sc_ref[...] = jax.nn.gelu(act) * gate_ref[...]


def _dot_exact01_left(m01, x):
    hi = x.astype(BF16)
    r1 = x - hi.astype(F32)
    mid = r1.astype(BF16)
    lo = (r1 - mid.astype(F32)).astype(BF16)
    return _dot(m01, hi) + _dot(m01, mid) + _dot(m01, lo)


def _peer_v_kernel(idx_ref, sc_ref, hres_ref, tab_hbm, y_ref, tab_vmem, tile_ref, smat_ref, sem):
    tb = sc_ref.shape[0]
    _load_table_once(tab_hbm, tab_vmem, sem)
    scb = sc_ref[...].astype(BF16)
    expanded = _dot(scb, _seg_matrix_t(N_ACTIVE, TILE_ROWS, ROW_CHUNKS)).astype(BF16)
    rep = _dot(_seg_matrix(tb * ROW_CHUNKS, ROW_CHUNKS, tb), expanded)
    sub = lax.broadcasted_iota(jnp.int32, (tb * ROW_CHUNKS, TILE_ROWS), 0) % ROW_CHUNKS
    lane = lax.broadcasted_iota(jnp.int32, (tb * ROW_CHUNKS, TILE_ROWS), 1) % ROW_CHUNKS
    smat_ref[...] = jnp.where(sub == lane, rep, 0.0).reshape(tb, ROW_CHUNKS, TILE_ROWS)

    def one_token(t, carry):
        _gather_tile(idx_ref, t, tab_vmem, tile_ref)
        rows = pltpu.bitcast(tile_ref[...], BF16)
        o = _dot(smat_ref[t].astype(BF16), rows)
        y_ref[t] = hres_ref[t] + o
        return carry

    lax.fori_loop(0, tb, one_token, 0)


def _peer_experts(idx, gate, xn2, hres, utab, vtab, tb=64):
    n = idx.shape[0]
    idx4 = idx * PACK_ROWS
    xn3 = xn2.reshape(n, ROW_CHUNKS, LANES)
    hres3 = hres.reshape(n, ROW_CHUNKS, LANES)
    smem_blk = pl.BlockSpec((tb, N_ACTIVE), lambda i: (i, 0), memory_space=pltpu.SMEM)
    row = pl.BlockSpec((tb, N_ACTIVE), lambda i: (i, 0))
    row3 = pl.BlockSpec((tb, ROW_CHUNKS, LANES), lambda i: (i, 0, 0))
    anyspace = pl.BlockSpec(memory_space=pl.ANY)
    table_scratch = [pltpu.VMEM((N_EXPERTS * PACK_ROWS, LANES), jnp.int32),
                     pltpu.VMEM((TILE_WORD_ROWS, LANES), jnp.int32),
                     pltpu.VMEM((tb, ROW_CHUNKS, TILE_ROWS), F32),
                     pltpu.SemaphoreType.DMA]
    sc = pl.pallas_call(
        _peer_u_kernel,
        grid=(n // tb,),
        in_specs=[smem_blk, row3, row, anyspace],
        out_specs=row,
        out_shape=jax.ShapeDtypeStruct((n, N_ACTIVE), F32),
        scratch_shapes=table_scratch,
        compiler_params=_cparams(("arbitrary",), VMEM_LIMIT),
        name="peer_u",
    )(idx4, xn3, gate, utab)
    y3 = pl.pallas_call(
        _peer_v_kernel,
        grid=(n // tb,),
        in_specs=[smem_blk, row, row3, anyspace],
        out_specs=row3,
        out_shape=jax.ShapeDtypeStruct((n, ROW_CHUNKS, LANES), F32),
        scratch_shapes=table_scratch,
        compiler_params=_cparams(("arbitrary",), VMEM_LIMIT),
        name="peer_v",
    )(idx4, sc, hres3, vtab)
    return y3.reshape(n, D_MODEL)


def kernel(x_prompt, x_sample, cache_conv, state_lru, cache_k, cache_v, norm1_g, w_in, conv_w, conv_b,
           rg_w_a, rg_b_a, rg_w_x, rg_b_x, rg_lambda, q_norm_g, k_norm_g, attn_sinks, w_branch_lru,
           w_branch_attn, w_out, norm2_g, peer_w_query, peer_sub_keys, expert_u, expert_v):
    assert norm1_g.shape[0] == 1, "one layer"
    batch, seq, _ = x_prompt.shape
    dbatch, dseq, _ = x_sample.shape
    assert dseq == SUBLANES and seq % 256 == 0
    n_p, n_s = batch * seq, dbatch * dseq
    l = 0
    x = jnp.concatenate([x_prompt.reshape(n_p, D_MODEL), x_sample.reshape(n_s, D_MODEL)], axis=0)

    xr, ggr, q, k, v, sga, sgb = _inproj(x, norm1_g[l], w_in[l].astype(BF16), q_norm_g[l], k_norm_g[l])

    lw = _lru_weights(conv_w[l], conv_b[l], rg_w_a[l], rg_b_a[l], rg_w_x[l], rg_b_x[l], rg_lambda[l])
    lru_p, hlast_p = _lru_prompt(xr[:n_p], ggr[:n_p], lw, batch, seq)
    lru_s, hlast_s = _lru_sample(xr[n_p:], ggr[n_p:], cache_conv[l], state_lru[l], lw)
    conv_p = xr[:n_p].reshape(batch, seq, LRU_WIDTH)[:, seq - (CONV_WIDTH - 1):]
    conv_s = jnp.concatenate([cache_conv[l], xr[n_p:].reshape(dbatch, dseq, LRU_WIDTH)],
                             axis=1)[:, -(CONV_WIDTH - 1):]

    attn_p = _attn_prompt(q[:n_p], k[:n_p], v[:n_p], attn_sinks[l], batch, seq)
    attn_s, k_s, v_s = _attn_sample(q[n_p:], k[n_p:], v[n_p:], cache_k[l], cache_v[l], attn_sinks[l])
    kv5 = (batch, WINDOW, N_KV_HEADS, HEAD_DIM)
    k_p = k[:n_p].reshape(batch, seq, KV_WIDTH)[:, seq - WINDOW:].reshape(kv5)
    v_p = v[:n_p].reshape(batch, seq, KV_WIDTH)[:, seq - WINDOW:].reshape(kv5)

    lru_out = jnp.concatenate([lru_p, lru_s], axis=0)
    attn = jnp.concatenate([attn_p, attn_s], axis=0)
    hres, xn2 = _post(x, lru_out, attn, sga, sgb, w_branch_lru[l].astype(BF16),
                      w_branch_attn[l].astype(BF16), w_out[l].astype(BF16), norm2_g[l])

    keys = peer_sub_keys[l].reshape(PEER_HEADS * 2, N_KEYS, KEY_DIM).astype(BF16)
    idx_t, gate_t = _route(xn2, peer_w_query[l].astype(BF16), keys)
    y = _peer_experts(idx_t.T, gate_t.T, xn2, hres, _pack_table(expert_u[l]), _pack_table(expert_v[l]))

    y_p = y[:n_p].reshape(batch, seq, D_MODEL)
    y_s = y[n_p:].reshape(dbatch, dseq, D_MODEL)
    st = lambda a: a[None]
    return (y_p, y_s, st(conv_p), st(hlast_p), st(k_p), st(v_p), st(conv_s), st(hlast_s), st(k_s), st(v_s))
```

```python
import functools

import jax
import jax.numpy as jnp
from jax import lax
from jax.experimental import pallas as pl
from jax.experimental.pallas import tpu as pltpu

D_MODEL = 1024
LRU_WIDTH = 1024
LRU_BLOCKS = 16
LRU_BLOCK = LRU_WIDTH // LRU_BLOCKS
CONV_WIDTH = 4
LRU_C = 8.0
N_HEADS = 16
N_KV_HEADS = 4
HEAD_DIM = 64
GROUP = N_HEADS // N_KV_HEADS
Q_WIDTH = N_HEADS * HEAD_DIM
KV_WIDTH = N_KV_HEADS * HEAD_DIM
WINDOW = 128
ATTN_BLOCK = 128
PAST_LEN = 16384
PEER_HEADS = 8
N_KEYS = 128
N_EXPERTS = N_KEYS * N_KEYS
KEY_DIM = 128
PEER_TOPK = 16
N_ACTIVE = PEER_HEADS * PEER_TOPK
IN_SPLITS = (LRU_WIDTH, LRU_WIDTH, Q_WIDTH, KV_WIDTH, KV_WIDTH, D_MODEL, D_MODEL)
IN_COLS = sum(IN_SPLITS)
EPS = 1e-6
NEG_INF = -1e30

LANES = 128
SUBLANES = 8
ROW_CHUNKS = D_MODEL // LANES
VMEM_LIMIT = 56 * 1024 * 1024

F32 = jnp.float32
BF16 = jnp.bfloat16


def _cparams(sem, vmem=None):
    return pltpu.CompilerParams(dimension_semantics=sem, vmem_limit_bytes=vmem)


def _dot(a, b):
    return jnp.dot(a, b, preferred_element_type=F32)


def _dot_nt(a, b):
    return lax.dot_general(a, b, (((1,), (1,)), ((), ())), preferred_element_type=F32)


def _dot_exact01(x, m01):
    hi = x.astype(BF16)
    r1 = x - hi.astype(F32)
    mid = r1.astype(BF16)
    lo = (r1 - mid.astype(F32)).astype(BF16)
    return _dot(hi, m01) + _dot(mid, m01) + _dot(lo, m01)


def _seg_matrix(width, seg, cols):
    c = lax.broadcasted_iota(jnp.int32, (width, cols), 0)
    h = lax.broadcasted_iota(jnp.int32, (width, cols), 1)
    return jnp.where(c // seg == h, 1.0, 0.0).astype(BF16)


def _seg_matrix_t(cols, width, seg):
    h = lax.broadcasted_iota(jnp.int32, (cols, width), 0)
    c = lax.broadcasted_iota(jnp.int32, (cols, width), 1)
    return jnp.where(c // seg == h, 1.0, 0.0).astype(BF16)


def _head_rmsnorm(t, gain_row, width):
    seg = _seg_matrix(width, HEAD_DIM, LANES)
    seg_t = _seg_matrix_t(LANES, width, HEAD_DIM)
    ssq = _dot_exact01(t * t, seg)
    inv = lax.rsqrt(ssq * (1.0 / HEAD_DIM) + EPS)
    inv_b = _dot_exact01(inv, seg_t)
    return t * inv_b * gain_row


def _inproj_kernel(x_ref, g1_ref, w_ref, qg_ref, kg_ref,
                   xr_ref, ggr_ref, q_ref, k_ref, v_ref, sga_ref, sgb_ref):
    x = x_ref[...]
    y = x * lax.rsqrt(jnp.mean(x * x, axis=-1, keepdims=True) + EPS)
    xn = (y * g1_ref[...]).astype(BF16)
    o = 0
    xr_ref[...] = _dot(xn, w_ref[:, o:o + LRU_WIDTH]); o += LRU_WIDTH
    ggr_ref[...] = jax.nn.gelu(_dot(xn, w_ref[:, o:o + LRU_WIDTH])); o += LRU_WIDTH
    q = _dot(xn, w_ref[:, o:o + Q_WIDTH]); o += Q_WIDTH
    q_ref[...] = (_head_rmsnorm(q, qg_ref[...], Q_WIDTH) * (HEAD_DIM ** -0.5)).astype(BF16)
    k = _dot(xn, w_ref[:, o:o + KV_WIDTH]); o += KV_WIDTH
    k_ref[...] = _head_rmsnorm(k, kg_ref[...], KV_WIDTH)
    v_ref[...] = _dot(xn, w_ref[:, o:o + KV_WIDTH]); o += KV_WIDTH
    sga_ref[...] = jax.nn.sigmoid(_dot(xn, w_ref[:, o:o + D_MODEL])); o += D_MODEL
    sgb_ref[...] = jax.nn.sigmoid(_dot(xn, w_ref[:, o:o + D_MODEL]))


def _inproj(x, norm1_g, w_in_bf16, q_norm_g, k_norm_g, tm=256):
    n = x.shape[0]
    assert n % tm == 0
    row = lambda w: pl.BlockSpec((tm, w), lambda i: (i, 0))
    full = lambda a: pl.BlockSpec(a.shape, lambda i: (0,) * a.ndim)
    g1 = norm1_g.reshape(1, D_MODEL)
    qg = jnp.tile(q_norm_g, N_HEADS).reshape(1, Q_WIDTH)
    kg = jnp.tile(k_norm_g, N_KV_HEADS).reshape(1, KV_WIDTH)
    outs = [(LRU_WIDTH, F32), (LRU_WIDTH, F32), (Q_WIDTH, BF16), (KV_WIDTH, F32), (KV_WIDTH, F32),
            (D_MODEL, F32), (D_MODEL, F32)]
    return pl.pallas_call(
        _inproj_kernel,
        grid=(n // tm,),
        in_specs=[row(D_MODEL), full(g1), full(w_in_bf16), full(qg), full(kg)],
        out_specs=[row(w) for w, _ in outs],
        out_shape=[jax.ShapeDtypeStruct((n, w), dt) for w, dt in outs],
        compiler_params=_cparams(("parallel",), VMEM_LIMIT),
        name="inproj",
    )(x, g1, w_in_bf16, qg, kg)


def _log_sigmoid(x):
    return jnp.minimum(x, 0.0) - jnp.log1p(jnp.exp(-jnp.abs(x)))


def _lru_gates(xc, wg_ref, ba_ref, bx_ref, lam_ref):
    g = _dot(xc.astype(BF16), wg_ref[...])
    r = jax.nn.sigmoid(g[:, :LRU_WIDTH] + ba_ref[...])
    i = jax.nn.sigmoid(g[:, LRU_WIDTH:] + bx_ref[...])
    log_a = LRU_C * r * _log_sigmoid(lam_ref[...])
    a = jnp.exp(log_a)
    b = jnp.sqrt(-jnp.tanh(log_a) * (a * a + 1.0)) * (i * xc)
    return a, b


def _shift_rows(x, s, tpos, fill):
    return jnp.where(tpos < s, fill, pltpu.roll(x, s, axis=0))


def _segment_scan(a, b, tpos, seg_len):
    s = 1
    while s < seg_len:
        a_prev = _shift_rows(a, s, tpos, 1.0)
        b_prev = _shift_rows(b, s, tpos, 0.0)
        b = a * b_prev + b
        a = a * a_prev
        s *= 2
    return a, b


def _lru_prompt_kernel(xr_ref, ggr_ref, cw_ref, cb_ref, wg_ref, ba_ref, bx_ref, lam_ref,
                       out_ref, hlast_ref, prev_ref, h_ref):
    tc = xr_ref.shape[0]

    @pl.when(pl.program_id(1) == 0)
    def _():
        prev_ref[...] = jnp.zeros_like(prev_ref)
        h_ref[...] = jnp.zeros_like(h_ref)

    xr = xr_ref[...]
    tpos = lax.broadcasted_iota(jnp.int32, (tc, LRU_WIDTH), 0)
    tpos8 = lax.broadcasted_iota(jnp.int32, (SUBLANES, LRU_WIDTH), 0)
    prev = prev_ref[...]
    xc = xr * cw_ref[CONV_WIDTH - 1:CONV_WIDTH, :] + cb_ref[...]
    for s in range(1, CONV_WIDTH):
        rolled = pltpu.roll(xr, s, axis=0)
        head = jnp.where(tpos8 < s, pltpu.roll(prev, s, axis=0), rolled[:SUBLANES])
        shifted = jnp.concatenate([head, rolled[SUBLANES:]], axis=0)
        xc = xc + shifted * cw_ref[CONV_WIDTH - 1 - s:CONV_WIDTH - s, :]
    a, b = _lru_gates(xc, wg_ref, ba_ref, bx_ref, lam_ref)
    a_cum, b_cum = _segment_scan(a, b, tpos, tc)
    h = a_cum * h_ref[0:1, :] + b_cum
    out_ref[...] = (h * ggr_ref[...]).astype(out_ref.dtype)
    last8 = h[tc - SUBLANES:]
    hlast_ref[...] = last8
    h_ref[...] = jnp.broadcast_to(last8[SUBLANES - 1:], h_ref.shape)
    prev_ref[...] = xr[tc - SUBLANES:]


def _lru_sample_kernel(xr_ref, ggr_ref, halo_ref, h0_ref, cw_ref, cb_ref, wg_ref, ba_ref, bx_ref,
                       lam_ref, out_ref, h_out_ref):
    rows = xr_ref.shape[0]
    xr = xr_ref[...]
    tpos = lax.broadcasted_iota(jnp.int32, (rows, LRU_WIDTH), 0) % SUBLANES
    xc = xr * cw_ref[CONV_WIDTH - 1:CONV_WIDTH, :] + cb_ref[...]
    for s in range(1, CONV_WIDTH):
        shifted = _shift_rows(xr, s, tpos, halo_ref[s - 1])
        xc = xc + shifted * cw_ref[CONV_WIDTH - 1 - s:CONV_WIDTH - s, :]
    a, b = _lru_gates(xc, wg_ref, ba_ref, bx_ref, lam_ref)
    b = b + a * h0_ref[...]
    _, h = _segment_scan(a, b, tpos, SUBLANES)
    out_ref[...] = (h * ggr_ref[...]).astype(out_ref.dtype)
    h_out_ref[...] = h


def _lru_weights(conv_w, conv_b, rg_w_a, rg_b_a, rg_w_x, rg_b_x, rg_lambda):
    eye = jnp.eye(LRU_BLOCKS, dtype=F32)
    bd = lambda w: jnp.einsum("njk,nm->njmk", w, eye).reshape(LRU_WIDTH, LRU_WIDTH)
    wg = jnp.concatenate([bd(rg_w_a), bd(rg_w_x)], axis=1).astype(BF16)
    r = lambda v: v.reshape(1, LRU_WIDTH)
    return conv_w, r(conv_b), wg, r(rg_b_a), r(rg_b_x), r(rg_lambda)


def _lru_prompt(xr, ggr, lw, batch, seq, tc=256):
    n = xr.shape[0]
    nt = seq // tc
    row = pl.BlockSpec((tc, LRU_WIDTH), lambda b, t: (b * nt + t, 0))
    full = lambda a: pl.BlockSpec(a.shape, lambda b, t: (0,) * a.ndim)
    out, hlast = pl.pallas_call(
        _lru_prompt_kernel,
        grid=(batch, nt),
        in_specs=[row, row] + [full(a) for a in lw],
        out_specs=[row, pl.BlockSpec((SUBLANES, LRU_WIDTH), lambda b, t: (b, 0))],
        out_shape=[jax.ShapeDtypeStruct((n, LRU_WIDTH), BF16),
                   jax.ShapeDtypeStruct((batch * SUBLANES, LRU_WIDTH), F32)],
        scratch_shapes=[pltpu.VMEM((SUBLANES, LRU_WIDTH), F32), pltpu.VMEM((SUBLANES, LRU_WIDTH), F32)],
        compiler_params=_cparams(("parallel", "arbitrary"), VMEM_LIMIT),
        name="lru_prompt",
    )(xr, ggr, *lw)
    return out, hlast.reshape(batch, SUBLANES, LRU_WIDTH)[:, SUBLANES - 1]


def _lru_sample(xr, ggr, conv_buf, h0, lw, rows_per_step=256):
    n = xr.shape[0]
    nb = n // SUBLANES
    halos = []
    for s in range(1, CONV_WIDTH):
        rows = conv_buf[:, CONV_WIDTH - 1 - s:, :]
        halos.append(jnp.concatenate(
            [rows, jnp.zeros((nb, SUBLANES - s, LRU_WIDTH), F32)], axis=1).reshape(n, LRU_WIDTH))
    halo = jnp.stack(halos)
    h0_rows = jnp.concatenate(
        [h0[:, None, :], jnp.zeros((nb, SUBLANES - 1, LRU_WIDTH), F32)], axis=1).reshape(n, LRU_WIDTH)
    r = rows_per_step
    row = pl.BlockSpec((r, LRU_WIDTH), lambda i: (i, 0))
    full = lambda a: pl.BlockSpec(a.shape, lambda i: (0,) * a.ndim)
    out, h = pl.pallas_call(
        _lru_sample_kernel,
        grid=(n // r,),
        in_specs=[row, row, pl.BlockSpec((CONV_WIDTH - 1, r, LRU_WIDTH), lambda i: (0, i, 0)), row]
        + [full(a) for a in lw],
        out_specs=[row, row],
        out_shape=[jax.ShapeDtypeStruct((n, LRU_WIDTH), BF16), jax.ShapeDtypeStruct((n, LRU_WIDTH), F32)],
        compiler_params=_cparams(("parallel",), VMEM_LIMIT),
        name="lru_sample",
    )(xr, ggr, halo, h0_rows, *lw)
    return out, h.reshape(nb, SUBLANES, LRU_WIDTH)[:, SUBLANES - 1]


def _softmax_sink_pv(s, sink, v):
    m = jnp.maximum(jnp.max(s, axis=-1, keepdims=True), sink)
    p = jnp.exp(s - m)
    denom = jnp.sum(p, axis=-1, keepdims=True) + jnp.exp(sink - m)
    return _dot(p.astype(BF16), v) / denom


def _attn_prompt_kernel(slopes_ref, sinks_ref, q_ref, kp_ref, kc_ref, vp_ref, vc_ref, o_ref):
    blk = pl.program_id(1)
    tq = ATTN_BLOCK
    qi = lax.broadcasted_iota(jnp.int32, (tq, 2 * tq), 0)
    kj = lax.broadcasted_iota(jnp.int32, (tq, 2 * tq), 1)
    dist = (tq + qi) - kj
    valid = (dist >= 0) & (dist <= WINDOW) & ((kj >= tq) | (blk > 0))
    distf = dist.astype(F32)
    k = jnp.concatenate([kp_ref[...], kc_ref[...]], axis=0).astype(BF16)
    v = jnp.concatenate([vp_ref[...], vc_ref[...]], axis=0).astype(BF16)
    for h in range(N_HEADS):
        g = h // GROUP
        qh = q_ref[:, h * HEAD_DIM:(h + 1) * HEAD_DIM]
        kg = k[:, g * HEAD_DIM:(g + 1) * HEAD_DIM]
        vg = v[:, g * HEAD_DIM:(g + 1) * HEAD_DIM]
        s = _dot_nt(qh, kg) - slopes_ref[h] * distf
        s = jnp.where(valid, s, NEG_INF)
        o = _softmax_sink_pv(s, sinks_ref[h], vg)
        o_ref[:, h * HEAD_DIM:(h + 1) * HEAD_DIM] = o.astype(o_ref.dtype)


def _alibi_slopes():
    return 2.0 ** (-8.0 * jnp.arange(1, N_HEADS + 1, dtype=F32) / N_HEADS)


def _attn_prompt(q, k, v, sinks, batch, seq):
    n = q.shape[0]
    nb = seq // ATTN_BLOCK
    smem = pl.BlockSpec(memory_space=pltpu.SMEM)
    cur = lambda w: pl.BlockSpec((ATTN_BLOCK, w), lambda b, i: (b * nb + i, 0))
    prv = lambda w: pl.BlockSpec((ATTN_BLOCK, w), lambda b, i: (b * nb + jnp.maximum(i - 1, 0), 0))
    return pl.pallas_call(
        _attn_prompt_kernel,
        grid=(batch, nb),
        in_specs=[smem, smem, cur(Q_WIDTH), prv(KV_WIDTH), cur(KV_WIDTH), prv(KV_WIDTH), cur(KV_WIDTH)],
        out_specs=cur(Q_WIDTH),
        out_shape=jax.ShapeDtypeStruct((n, Q_WIDTH), BF16),
        compiler_params=_cparams(("parallel", "arbitrary"), VMEM_LIMIT),
        name="attn_prompt",
    )(_alibi_slopes(), sinks.astype(F32), q, k, k, v, v)


def _attn_sample_kernel(slopes_ref, sinks_ref, q_ref, kn_ref, vn_ref, ck_ref, cv_ref,
                        o_ref, ko_ref, vo_ref):
    nseq = ck_ref.shape[0]
    t = SUBLANES
    keys = WINDOW + t
    qi = lax.broadcasted_iota(jnp.int32, (t, keys), 0)
    kj = lax.broadcasted_iota(jnp.int32, (t, keys), 1)
    dist = (WINDOW + qi) - kj
    valid = (dist >= 0) & (dist <= WINDOW)
    distf = dist.astype(F32)

    def one_seq(b, carry):
        kn = kn_ref[b]
        vn = vn_ref[b]
        ck = ck_ref[b]
        cv = cv_ref[b]
        kall = jnp.concatenate([ck, kn], axis=0)
        vall = jnp.concatenate([cv, vn], axis=0)
        ko_ref[b] = kall[t:]
        vo_ref[b] = vall[t:]
        kb = kall.astype(BF16)
        vb = vall.astype(BF16)
        q = q_ref[b]
        outs = []
        for h in range(N_HEADS):
            g = h // GROUP
            qh = q[:, h * HEAD_DIM:(h + 1) * HEAD_DIM]
            s = _dot_nt(qh, kb[:, g * HEAD_DIM:(g + 1) * HEAD_DIM]) - slopes_ref[h] * distf
            s = jnp.where(valid, s, NEG_INF)
            outs.append(_softmax_sink_pv(s, sinks_ref[h], vb[:, g * HEAD_DIM:(g + 1) * HEAD_DIM]))
        o_ref[b] = jnp.concatenate(outs, axis=-1).astype(o_ref.dtype)
        return carry

    lax.fori_loop(0, nseq, one_seq, 0)


def _attn_sample(q, k_new, v_new, cache_k, cache_v, sinks, seqs_per_step=8):
    nb = cache_k.shape[0]
    t = SUBLANES
    sb = seqs_per_step
    smem = pl.BlockSpec(memory_space=pltpu.SMEM)
    blk = lambda r, w: pl.BlockSpec((sb, r, w), lambda i: (i, 0, 0))
    ck = cache_k.reshape(nb, WINDOW, KV_WIDTH)
    cv = cache_v.reshape(nb, WINDOW, KV_WIDTH)
    o, ko, vo = pl.pallas_call(
        _attn_sample_kernel,
        grid=(nb // sb,),
        in_specs=[smem, smem, blk(t, Q_WIDTH), blk(t, KV_WIDTH), blk(t, KV_WIDTH),
                  blk(WINDOW, KV_WIDTH), blk(WINDOW, KV_WIDTH)],
        out_specs=[blk(t, Q_WIDTH), blk(WINDOW, KV_WIDTH), blk(WINDOW, KV_WIDTH)],
        out_shape=[jax.ShapeDtypeStruct((nb, t, Q_WIDTH), BF16),
                   jax.ShapeDtypeStruct((nb, WINDOW, KV_WIDTH), F32),
                   jax.ShapeDtypeStruct((nb, WINDOW, KV_WIDTH), F32)],
        compiler_params=_cparams(("parallel",), VMEM_LIMIT),
        name="attn_sample",
    )(_alibi_slopes(), sinks.astype(F32), q.reshape(nb, t, Q_WIDTH), k_new.reshape(nb, t, KV_WIDTH),
      v_new.reshape(nb, t, KV_WIDTH), ck, cv)
    shape5 = (nb, WINDOW, N_KV_HEADS, HEAD_DIM)
    return o.reshape(nb * t, Q_WIDTH), ko.reshape(shape5), vo.reshape(shape5)


def _post_kernel(x_ref, lru_ref, attn_ref, sga_ref, sgb_ref, wl_ref, wa_ref, wo_ref, g2_ref,
                 hres_ref, xn2_ref):
    merged = sga_ref[...] * _dot(lru_ref[...], wl_ref[...]) + sgb_ref[...] * _dot(attn_ref[...], wa_ref[...])
    hres = x_ref[...] + _dot(merged.astype(BF16), wo_ref[...])
    hres_ref[...] = hres
    y = hres * lax.rsqrt(jnp.mean(hres * hres, axis=-1, keepdims=True) + EPS)
    xn2_ref[...] = y * g2_ref[...]


def _post(x, lru_out, attn, sga, sgb, wl, wa, wo, norm2_g, tm=256):
    n = x.shape[0]
    row = pl.BlockSpec((tm, D_MODEL), lambda i: (i, 0))
    full = lambda a: pl.BlockSpec(a.shape, lambda i: (0,) * a.ndim)
    g2 = norm2_g.reshape(1, D_MODEL)
    return pl.pallas_call(
        _post_kernel,
        grid=(n // tm,),
        in_specs=[row] * 5 + [full(wl), full(wa), full(wo), full(g2)],
        out_specs=[row, row],
        out_shape=[jax.ShapeDtypeStruct((n, D_MODEL), F32)] * 2,
        compiler_params=_cparams(("parallel",), VMEM_LIMIT),
        name="post",
    )(x, lru_out, attn, sga, sgb, wl, wa, wo, g2)


def _topk_rows(s, k):
    rows = s.shape[0]
    iota = lax.broadcasted_iota(jnp.int32, s.shape, 0)
    vals, idxs = [], []
    for _ in range(k):
        m = jnp.max(s, axis=0, keepdims=True)
        sel = jnp.min(jnp.where(s == m, iota, rows), axis=0, keepdims=True)
        vals.append(m)
        idxs.append(sel)
        s = jnp.where(iota == sel, -jnp.inf, s)
    return jnp.concatenate(vals, axis=0), jnp.concatenate(idxs, axis=0)


def _pick_rows(table, which):
    out = jnp.zeros(which.shape, table.dtype)
    for a in range(table.shape[0]):
        out = out + jnp.where(which == a, table[a:a + 1, :], 0)
    return out


def _route_kernel(xn_ref, wq_ref, keys_ref, idx_ref, gate_ref):
    qr = _dot(xn_ref[...].astype(BF16), wq_ref[...]).astype(BF16)
    for h in range(PEER_HEADS):
        tops = []
        for p in range(2):
            c = h * 2 + p
            st = _dot_nt(keys_ref[c], qr[:, c * KEY_DIM:(c + 1) * KEY_DIM])
            tops.append(_topk_rows(st, PEER_TOPK))
        (s1, i1), (s2, i2) = tops
        cand = jnp.concatenate([s1[a:a + 1, :] + s2 for a in range(PEER_TOPK)], axis=0)
        best, pos = _topk_rows(cand, PEER_TOPK)
        e1 = _pick_rows(i1, pos // PEER_TOPK)
        e2 = _pick_rows(i2, pos % PEER_TOPK)
        idx_ref[h * PEER_TOPK:(h + 1) * PEER_TOPK, :] = e1 * N_KEYS + e2
        ex = jnp.exp(best - best[0:1, :])
        gate_ref[h * PEER_TOPK:(h + 1) * PEER_TOPK, :] = ex / jnp.sum(ex, axis=0, keepdims=True)


def _route(xn2, wq_bf16, keys_bf16, tm=128):
    n = xn2.shape[0]
    full = lambda a: pl.BlockSpec(a.shape, lambda i: (0,) * a.ndim)
    col = pl.BlockSpec((N_ACTIVE, tm), lambda i: (0, i))
    return pl.pallas_call(
        _route_kernel,
        grid=(n // tm,),
        in_specs=[pl.BlockSpec((tm, D_MODEL), lambda i: (i, 0)), full(wq_bf16), full(keys_bf16)],
        out_specs=[col, col],
        out_shape=[jax.ShapeDtypeStruct((N_ACTIVE, n), jnp.int32), jax.ShapeDtypeStruct((N_ACTIVE, n), F32)],
        compiler_params=_cparams(("parallel",), VMEM_LIMIT),
        name="route",
    )(xn2, wq_bf16, keys_bf16)


PACK_ROWS = ROW_CHUNKS // 2
TILE_WORD_ROWS = N_ACTIVE * PACK_ROWS
TILE_ROWS = N_ACTIVE * ROW_CHUNKS


def _pack_table(w):
    wb = w.astype(BF16).reshape(N_EXPERTS, PACK_ROWS, 2, LANES)
    wb = jnp.transpose(wb, (0, 1, 3, 2))
    return lax.bitcast_convert_type(wb, jnp.int32).reshape(N_EXPERTS * PACK_ROWS, LANES)


def _load_table_once(tab_hbm, tab_vmem, sem):
    @pl.when(pl.program_id(0) == 0)
    def _():
        cp = pltpu.make_async_copy(tab_hbm, tab_vmem, sem)
        cp.start()
        cp.wait()


def _gather_tile(idx_ref, t, tab_vmem, tile_ref):
    for j in range(N_ACTIVE):
        e4 = pl.multiple_of(idx_ref[t, j], PACK_ROWS)
        tile_ref[j * PACK_ROWS:(j + 1) * PACK_ROWS, :] = tab_vmem[pl.ds(e4, PACK_ROWS), :]


def _chunk_diag_mask():
    sub = lax.broadcasted_iota(jnp.int32, (ROW_CHUNKS, TILE_ROWS), 0)
    lane = lax.broadcasted_iota(jnp.int32, (ROW_CHUNKS, TILE_ROWS), 1)
    return (lane % ROW_CHUNKS) == sub


def _token_pipeline(tb, idx_ref, tab_vmem, tile_a, tile_b, compute):
    _gather_tile(idx_ref, 0, tab_vmem, tile_a)
    _gather_tile(idx_ref, 1, tab_vmem, tile_b)

    def two_tokens(i, carry):
        t = 2 * i
        compute(t, tile_a)
        _gather_tile(idx_ref, jnp.minimum(t + 2, tb - 1), tab_vmem, tile_a)
        compute(t + 1, tile_b)
        _gather_tile(idx_ref, jnp.minimum(t + 3, tb - 1), tab_vmem, tile_b)
        return carry

    lax.fori_loop(0, tb // 2, two_tokens, 0)


def _peer_u_kernel(idx_ref, xn_ref, gate_ref, tab_hbm, sc_ref, tab_vmem, tile_a, tile_b, part_ref, sem):
    tb = xn_ref.shape[0]
    _load_table_once(tab_hbm, tab_vmem, sem)
    mask = _chunk_diag_mask()

    def compute(t, tile_ref):
        rows = pltpu.bitcast(tile_ref[...], BF16)
        xw = xn_ref[t].astype(BF16)
        r = _dot_nt(xw, rows)
        part_ref[t] = jnp.where(mask, r, 0.0)

    _token_pipeline(tb, idx_ref, tab_vmem, tile_a, tile_b, compute)
    part = part_ref[...].reshape(tb * ROW_CHUNKS, TILE_ROWS)
    per_chunk = _dot_exact01(part, _seg_matrix(TILE_ROWS, ROW_CHUNKS, N_ACTIVE))
    act = _dot_exact01_left(_seg_matrix_t(tb, tb * ROW_CHUNKS, ROW_CHUNKS), per_chunk)
    sc_ref[...] = jax.nn.gelu(act) * gate_ref[...]


def _dot_exact01_left(m01, x):
    hi = x.astype(BF16)
    r1 = x - hi.astype(F32)
    mid = r1.astype(BF16)
    lo = (r1 - mid.astype(F32)).astype(BF16)
    return _dot(m01, hi) + _dot(m01, mid) + _dot(m01, lo)


def _peer_v_kernel(idx_ref, sc_ref, hres_ref, tab_hbm, y_ref, tab_vmem, tile_a, tile_b, smat_ref, sem):
    tb = sc_ref.shape[0]
    _load_table_once(tab_hbm, tab_vmem, sem)
    scb = sc_ref[...].astype(BF16)
    expanded = _dot(scb, _seg_matrix_t(N_ACTIVE, TILE_ROWS, ROW_CHUNKS)).astype(BF16)
    rep = _dot(_seg_matrix(tb * ROW_CHUNKS, ROW_CHUNKS, tb), expanded)
    sub = lax.broadcasted_iota(jnp.int32, (tb * ROW_CHUNKS, TILE_ROWS), 0) % ROW_CHUNKS
    lane = lax.broadcasted_iota(jnp.int32, (tb * ROW_CHUNKS, TILE_ROWS), 1) % ROW_CHUNKS
    smat_ref[...] = jnp.where(sub == lane, rep, 0.0).reshape(tb, ROW_CHUNKS, TILE_ROWS)

    def compute(t, tile_ref):
        rows = pltpu.bitcast(tile_ref[...], BF16)
        o = _dot(smat_ref[t].astype(BF16), rows)
        y_ref[t] = hres_ref[t] + o

    _token_pipeline(tb, idx_ref, tab_vmem, tile_a, tile_b, compute)


def _peer_experts(idx, gate, xn2, hres, utab, vtab, tb=64):
    n = idx.shape[0]
    idx4 = idx * PACK_ROWS
    xn3 = xn2.reshape(n, ROW_CHUNKS, LANES)
    hres3 = hres.reshape(n, ROW_CHUNKS, LANES)
    smem_blk = pl.BlockSpec((tb, N_ACTIVE), lambda i: (i, 0), memory_space=pltpu.SMEM)
    row = pl.BlockSpec((tb, N_ACTIVE), lambda i: (i, 0))
    row3 = pl.BlockSpec((tb, ROW_CHUNKS, LANES), lambda i: (i, 0, 0))
    anyspace = pl.BlockSpec(memory_space=pl.ANY)
    table_scratch = [pltpu.VMEM((N_EXPERTS * PACK_ROWS, LANES), jnp.int32),
                     pltpu.VMEM((TILE_WORD_ROWS, LANES), jnp.int32),
                     pltpu.VMEM((TILE_WORD_ROWS, LANES), jnp.int32),
                     pltpu.VMEM((tb, ROW_CHUNKS, TILE_ROWS), F32),
                     pltpu.SemaphoreType.DMA]
    sc = pl.pallas_call(
        _peer_u_kernel,
        grid=(n // tb,),
        in_specs=[smem_blk, row3, row, anyspace],
        out_specs=row,
        out_shape=jax.ShapeDtypeStruct((n, N_ACTIVE), F32),
        scratch_shapes=table_scratch,
        compiler_params=_cparams(("arbitrary",), VMEM_LIMIT),
        name="peer_u",
    )(idx4, xn3, gate, utab)
    y3 = pl.pallas_call(
        _peer_v_kernel,
        grid=(n // tb,),
        in_specs=[smem_blk, row, row3, anyspace],
        out_specs=row3,
        out_shape=jax.ShapeDtypeStruct((n, ROW_CHUNKS, LANES), F32),
        scratch_shapes=table_scratch,
        compiler_params=_cparams(("arbitrary",), VMEM_LIMIT),
        name="peer_v",
    )(idx4, sc, hres3, vtab)
    return y3.reshape(n, D_MODEL)


def kernel(x_prompt, x_sample, cache_conv, state_lru, cache_k, cache_v, norm1_g, w_in, conv_w, conv_b,
           rg_w_a, rg_b_a, rg_w_x, rg_b_x, rg_lambda, q_norm_g, k_norm_g, attn_sinks, w_branch_lru,
           w_branch_attn, w_out, norm2_g, peer_w_query, peer_sub_keys, expert_u, expert_v):
    assert norm1_g.shape[0] == 1, "one layer"
    batch, seq, _ = x_prompt.shape
    dbatch, dseq, _ = x_sample.shape
    assert dseq == SUBLANES and seq % 256 == 0
    n_p, n_s = batch * seq, dbatch * dseq
    l = 0
    x = jnp.concatenate([x_prompt.reshape(n_p, D_MODEL), x_sample.reshape(n_s, D_MODEL)], axis=0)

    xr, ggr, q, k, v, sga, sgb = _inproj(x, norm1_g[l], w_in[l].astype(BF16), q_norm_g[l], k_norm_g[l])

    lw = _lru_weights(conv_w[l], conv_b[l], rg_w_a[l], rg_b_a[l], rg_w_x[l], rg_b_x[l], rg_lambda[l])
    lru_p, hlast_p = _lru_prompt(xr[:n_p], ggr[:n_p], lw, batch, seq)
    lru_s, hlast_s = _lru_sample(xr[n_p:], ggr[n_p:], cache_conv[l], state_lru[l], lw)
    conv_p = xr[:n_p].reshape(batch, seq, LRU_WIDTH)[:, seq - (CONV_WIDTH - 1):]
    conv_s = jnp.concatenate([cache_conv[l], xr[n_p:].reshape(dbatch, dseq, LRU_WIDTH)],
                             axis=1)[:, -(CONV_WIDTH - 1):]

    attn_p = _attn_prompt(q[:n_p], k[:n_p], v[:n_p], attn_sinks[l], batch, seq)
    attn_s, k_s, v_s = _attn_sample(q[n_p:], k[n_p:], v[n_p:], cache_k[l], cache_v[l], attn_sinks[l])
    kv5 = (batch, WINDOW, N_KV_HEADS, HEAD_DIM)
    k_p = k[:n_p].reshape(batch, seq, KV_WIDTH)[:, seq - WINDOW:].reshape(kv5)
    v_p = v[:n_p].reshape(batch, seq, KV_WIDTH)[:, seq - WINDOW:].reshape(kv5)

    lru_out = jnp.concatenate([lru_p, lru_s], axis=0)
    attn = jnp.concatenate([attn_p, attn_s], axis=0)
    hres, xn2 = _post(x, lru_out, attn, sga, sgb, w_branch_lru[l].astype(BF16),
                      w_branch_attn[l].astype(BF16), w_out[l].astype(BF16), norm2_g[l])

    keys = peer_sub_keys[l].reshape(PEER_HEADS * 2, N_KEYS, KEY_DIM).astype(BF16)
    idx_t, gate_t = _route(xn2, peer_w_query[l].astype(BF16), keys)
    y = _peer_experts(idx_t.T, gate_t.T, xn2, hres, _pack_table(expert_u[l]), _pack_table(expert_v[l]))

    y_p = y[:n_p].reshape(batch, seq, D_MODEL)
    y_s = y[n_p:].reshape(dbatch, dseq, D_MODEL)
    st = lambda a: a[None]
    return (y_p, y_s, st(conv_p), st(hlast_p), st(k_p), st(v_p), st(conv_s), st(hlast_s), st(k_s), st(v_s))
```

```python
import functools

import jax
import jax.numpy as jnp
from jax import lax
from jax.experimental import pallas as pl
from jax.experimental.pallas import tpu as pltpu

D_MODEL = 1024
LRU_WIDTH = 1024
LRU_BLOCKS = 16
LRU_BLOCK = LRU_WIDTH // LRU_BLOCKS
CONV_WIDTH = 4
LRU_C = 8.0
N_HEADS = 16
N_KV_HEADS = 4
HEAD_DIM = 64
GROUP = N_HEADS // N_KV_HEADS
Q_WIDTH = N_HEADS * HEAD_DIM
KV_WIDTH = N_KV_HEADS * HEAD_DIM
WINDOW = 128
ATTN_BLOCK = 128
PAST_LEN = 16384
PEER_HEADS = 8
N_KEYS = 128
N_EXPERTS = N_KEYS * N_KEYS
KEY_DIM = 128
PEER_TOPK = 16
N_ACTIVE = PEER_HEADS * PEER_TOPK
IN_SPLITS = (LRU_WIDTH, LRU_WIDTH, Q_WIDTH, KV_WIDTH, KV_WIDTH, D_MODEL, D_MODEL)
IN_COLS = sum(IN_SPLITS)
EPS = 1e-6
NEG_INF = -1e30

LANES = 128
SUBLANES = 8
ROW_CHUNKS = D_MODEL // LANES
VMEM_LIMIT = 56 * 1024 * 1024

F32 = jnp.float32
BF16 = jnp.bfloat16


def _cparams(sem, vmem=None):
    return pltpu.CompilerParams(dimension_semantics=sem, vmem_limit_bytes=vmem)


def _dot(a, b):
    return jnp.dot(a, b, preferred_element_type=F32)


def _dot_nt(a, b):
    return lax.dot_general(a, b, (((1,), (1,)), ((), ())), preferred_element_type=F32)


def _dot_exact01(x, m01):
    hi = x.astype(BF16)
    r1 = x - hi.astype(F32)
    mid = r1.astype(BF16)
    lo = (r1 - mid.astype(F32)).astype(BF16)
    return _dot(hi, m01) + _dot(mid, m01) + _dot(lo, m01)


def _seg_matrix(width, seg, cols):
    c = lax.broadcasted_iota(jnp.int32, (width, cols), 0)
    h = lax.broadcasted_iota(jnp.int32, (width, cols), 1)
    return jnp.where(c // seg == h, 1.0, 0.0).astype(BF16)


def _seg_matrix_t(cols, width, seg):
    h = lax.broadcasted_iota(jnp.int32, (cols, width), 0)
    c = lax.broadcasted_iota(jnp.int32, (cols, width), 1)
    return jnp.where(c // seg == h, 1.0, 0.0).astype(BF16)


def _head_rmsnorm(t, gain_row, width):
    seg = _seg_matrix(width, HEAD_DIM, LANES)
    seg_t = _seg_matrix_t(LANES, width, HEAD_DIM)
    ssq = _dot_exact01(t * t, seg)
    inv = lax.rsqrt(ssq * (1.0 / HEAD_DIM) + EPS)
    inv_b = _dot_exact01(inv, seg_t)
    return t * inv_b * gain_row


def _inproj_kernel(x_ref, g1_ref, w_ref, qg_ref, kg_ref,
                   xr_ref, ggr_ref, q_ref, k_ref, v_ref, sga_ref, sgb_ref):
    x = x_ref[...]
    y = x * lax.rsqrt(jnp.mean(x * x, axis=-1, keepdims=True) + EPS)
    xn = (y * g1_ref[...]).astype(BF16)
    o = 0
    xr_ref[...] = _dot(xn, w_ref[:, o:o + LRU_WIDTH]); o += LRU_WIDTH
    ggr_ref[...] = jax.nn.gelu(_dot(xn, w_ref[:, o:o + LRU_WIDTH])); o += LRU_WIDTH
    q = _dot(xn, w_ref[:, o:o + Q_WIDTH]); o += Q_WIDTH
    q_ref[...] = (_head_rmsnorm(q, qg_ref[...], Q_WIDTH) * (HEAD_DIM ** -0.5)).astype(BF16)
    k = _dot(xn, w_ref[:, o:o + KV_WIDTH]); o += KV_WIDTH
    k_ref[...] = _head_rmsnorm(k, kg_ref[...], KV_WIDTH)
    v_ref[...] = _dot(xn, w_ref[:, o:o + KV_WIDTH]); o += KV_WIDTH
    sga_ref[...] = jax.nn.sigmoid(_dot(xn, w_ref[:, o:o + D_MODEL])); o += D_MODEL
    sgb_ref[...] = jax.nn.sigmoid(_dot(xn, w_ref[:, o:o + D_MODEL]))


def _inproj(x, norm1_g, w_in_bf16, q_norm_g, k_norm_g, tm=256):
    n = x.shape[0]
    assert n % tm == 0
    row = lambda w: pl.BlockSpec((tm, w), lambda i: (i, 0))
    full = lambda a: pl.BlockSpec(a.shape, lambda i: (0,) * a.ndim)
    g1 = norm1_g.reshape(1, D_MODEL)
    qg = jnp.tile(q_norm_g, N_HEADS).reshape(1, Q_WIDTH)
    kg = jnp.tile(k_norm_g, N_KV_HEADS).reshape(1, KV_WIDTH)
    outs = [(LRU_WIDTH, F32), (LRU_WIDTH, F32), (Q_WIDTH, BF16), (KV_WIDTH, F32), (KV_WIDTH, F32),
            (D_MODEL, F32), (D_MODEL, F32)]
    return pl.pallas_call(
        _inproj_kernel,
        grid=(n // tm,),
        in_specs=[row(D_MODEL), full(g1), full(w_in_bf16), full(qg), full(kg)],
        out_specs=[row(w) for w, _ in outs],
        out_shape=[jax.ShapeDtypeStruct((n, w), dt) for w, dt in outs],
        compiler_params=_cparams(("parallel",), VMEM_LIMIT),
        name="inproj",
    )(x, g1, w_in_bf16, qg, kg)


def _log_sigmoid(x):
    return jnp.minimum(x, 0.0) - jnp.log1p(jnp.exp(-jnp.abs(x)))


def _lru_gates(xc, wg_ref, ba_ref, bx_ref, lam_ref):
    g = _dot(xc.astype(BF16), wg_ref[...])
    r = jax.nn.sigmoid(g[:, :LRU_WIDTH] + ba_ref[...])
    i = jax.nn.sigmoid(g[:, LRU_WIDTH:] + bx_ref[...])
    log_a = LRU_C * r * _log_sigmoid(lam_ref[...])
    a = jnp.exp(log_a)
    b = jnp.sqrt(-jnp.tanh(log_a) * (a * a + 1.0)) * (i * xc)
    return a, b


def _shift_rows(x, s, tpos, fill):
    return jnp.where(tpos < s, fill, pltpu.roll(x, s, axis=0))


def _segment_scan(a, b, tpos, seg_len):
    s = 1
    while s < seg_len:
        a_prev = _shift_rows(a, s, tpos, 1.0)
        b_prev = _shift_rows(b, s, tpos, 0.0)
        b = a * b_prev + b
        a = a * a_prev
        s *= 2
    return a, b


def _lru_prompt_kernel(xr_ref, ggr_ref, cw_ref, cb_ref, wg_ref, ba_ref, bx_ref, lam_ref,
                       out_ref, hlast_ref, prev_ref, h_ref):
    tc = xr_ref.shape[0]

    @pl.when(pl.program_id(1) == 0)
    def _():
        prev_ref[...] = jnp.zeros_like(prev_ref)
        h_ref[...] = jnp.zeros_like(h_ref)

    xr = xr_ref[...]
    tpos = lax.broadcasted_iota(jnp.int32, (tc, LRU_WIDTH), 0)
    tpos8 = lax.broadcasted_iota(jnp.int32, (SUBLANES, LRU_WIDTH), 0)
    prev = prev_ref[...]
    xc = xr * cw_ref[CONV_WIDTH - 1:CONV_WIDTH, :] + cb_ref[...]
    for s in range(1, CONV_WIDTH):
        rolled = pltpu.roll(xr, s, axis=0)
        head = jnp.where(tpos8 < s, pltpu.roll(prev, s, axis=0), rolled[:SUBLANES])
        shifted = jnp.concatenate([head, rolled[SUBLANES:]], axis=0)
        xc = xc + shifted * cw_ref[CONV_WIDTH - 1 - s:CONV_WIDTH - s, :]
    a, b = _lru_gates(xc, wg_ref, ba_ref, bx_ref, lam_ref)
    a_cum, b_cum = _segment_scan(a, b, tpos, tc)
    h = a_cum * h_ref[0:1, :] + b_cum
    out_ref[...] = (h * ggr_ref[...]).astype(out_ref.dtype)
    last8 = h[tc - SUBLANES:]
    hlast_ref[...] = last8
    h_ref[...] = jnp.broadcast_to(last8[SUBLANES - 1:], h_ref.shape)
    prev_ref[...] = xr[tc - SUBLANES:]


def _lru_sample_kernel(xr_ref, ggr_ref, halo_ref, h0_ref, cw_ref, cb_ref, wg_ref, ba_ref, bx_ref,
                       lam_ref, out_ref, h_out_ref):
    rows = xr_ref.shape[0]
    xr = xr_ref[...]
    tpos = lax.broadcasted_iota(jnp.int32, (rows, LRU_WIDTH), 0) % SUBLANES
    xc = xr * cw_ref[CONV_WIDTH - 1:CONV_WIDTH, :] + cb_ref[...]
    for s in range(1, CONV_WIDTH):
        shifted = _shift_rows(xr, s, tpos, halo_ref[s - 1])
        xc = xc + shifted * cw_ref[CONV_WIDTH - 1 - s:CONV_WIDTH - s, :]
    a, b = _lru_gates(xc, wg_ref, ba_ref, bx_ref, lam_ref)
    b = b + a * h0_ref[...]
    _, h = _segment_scan(a, b, tpos, SUBLANES)
    out_ref[...] = (h * ggr_ref[...]).astype(out_ref.dtype)
    h_out_ref[...] = h


def _lru_weights(conv_w, conv_b, rg_w_a, rg_b_a, rg_w_x, rg_b_x, rg_lambda):
    eye = jnp.eye(LRU_BLOCKS, dtype=F32)
    bd = lambda w: jnp.einsum("njk,nm->njmk", w, eye).reshape(LRU_WIDTH, LRU_WIDTH)
    wg = jnp.concatenate([bd(rg_w_a), bd(rg_w_x)], axis=1).astype(BF16)
    r = lambda v: v.reshape(1, LRU_WIDTH)
    return conv_w, r(conv_b), wg, r(rg_b_a), r(rg_b_x), r(rg_lambda)


def _lru_prompt(xr, ggr, lw, batch, seq, tc=256):
    n = xr.shape[0]
    nt = seq // tc
    row = pl.BlockSpec((tc, LRU_WIDTH), lambda b, t: (b * nt + t, 0))
    full = lambda a: pl.BlockSpec(a.shape, lambda b, t: (0,) * a.ndim)
    out, hlast = pl.pallas_call(
        _lru_prompt_kernel,
        grid=(batch, nt),
        in_specs=[row, row] + [full(a) for a in lw],
        out_specs=[row, pl.BlockSpec((SUBLANES, LRU_WIDTH), lambda b, t: (b, 0))],
        out_shape=[jax.ShapeDtypeStruct((n, LRU_WIDTH), BF16),
                   jax.ShapeDtypeStruct((batch * SUBLANES, LRU_WIDTH), F32)],
        scratch_shapes=[pltpu.VMEM((SUBLANES, LRU_WIDTH), F32), pltpu.VMEM((SUBLANES, LRU_WIDTH), F32)],
        compiler_params=_cparams(("parallel", "arbitrary"), VMEM_LIMIT),
        name="lru_prompt",
    )(xr, ggr, *lw)
    return out, hlast.reshape(batch, SUBLANES, LRU_WIDTH)[:, SUBLANES - 1]


def _lru_sample(xr, ggr, conv_buf, h0, lw, rows_per_step=256):
    n = xr.shape[0]
    nb = n // SUBLANES
    halos = []
    for s in range(1, CONV_WIDTH):
        rows = conv_buf[:, CONV_WIDTH - 1 - s:, :]
        halos.append(jnp.concatenate(
            [rows, jnp.zeros((nb, SUBLANES - s, LRU_WIDTH), F32)], axis=1).reshape(n, LRU_WIDTH))
    halo = jnp.stack(halos)
    h0_rows = jnp.concatenate(
        [h0[:, None, :], jnp.zeros((nb, SUBLANES - 1, LRU_WIDTH), F32)], axis=1).reshape(n, LRU_WIDTH)
    r = rows_per_step
    row = pl.BlockSpec((r, LRU_WIDTH), lambda i: (i, 0))
    full = lambda a: pl.BlockSpec(a.shape, lambda i: (0,) * a.ndim)
    out, h = pl.pallas_call(
        _lru_sample_kernel,
        grid=(n // r,),
        in_specs=[row, row, pl.BlockSpec((CONV_WIDTH - 1, r, LRU_WIDTH), lambda i: (0, i, 0)), row]
        + [full(a) for a in lw],
        out_specs=[row, row],
        out_shape=[jax.ShapeDtypeStruct((n, LRU_WIDTH), BF16), jax.ShapeDtypeStruct((n, LRU_WIDTH), F32)],
        compiler_params=_cparams(("parallel",), VMEM_LIMIT),
        name="lru_sample",
    )(xr, ggr, halo, h0_rows, *lw)
    return out, h.reshape(nb, SUBLANES, LRU_WIDTH)[:, SUBLANES - 1]


def _attend_heads(q, k, v, slopes_ref, sinks_ref, distf, valid):
    heads = range(N_HEADS)
    col = lambda a, i: a[:, i * HEAD_DIM:(i + 1) * HEAD_DIM]
    ss = [jnp.where(valid, _dot_nt(col(q, h), col(k, h // GROUP)) - slopes_ref[h] * distf, NEG_INF)
          for h in heads]
    ms = [jnp.maximum(jnp.max(ss[h], axis=-1, keepdims=True), sinks_ref[h]) for h in heads]
    ps = [jnp.exp(ss[h] - ms[h]) for h in heads]
    dens = [jnp.sum(ps[h], axis=-1, keepdims=True) + jnp.exp(sinks_ref[h] - ms[h]) for h in heads]
    outs = [_dot(ps[h].astype(BF16), col(v, h // GROUP)) / dens[h] for h in heads]
    return jnp.concatenate(outs, axis=-1)


def _attn_prompt_kernel(slopes_ref, sinks_ref, q_ref, kp_ref, kc_ref, vp_ref, vc_ref, o_ref):
    blk = pl.program_id(1)
    tq = ATTN_BLOCK
    qi = lax.broadcasted_iota(jnp.int32, (tq, 2 * tq), 0)
    kj = lax.broadcasted_iota(jnp.int32, (tq, 2 * tq), 1)
    dist = (tq + qi) - kj
    valid = (dist >= 0) & (dist <= WINDOW) & ((kj >= tq) | (blk > 0))
    distf = dist.astype(F32)
    k = jnp.concatenate([kp_ref[...], kc_ref[...]], axis=0).astype(BF16)
    v = jnp.concatenate([vp_ref[...], vc_ref[...]], axis=0).astype(BF16)
    o_ref[...] = _attend_heads(q_ref[...], k, v, slopes_ref, sinks_ref, distf, valid).astype(o_ref.dtype)


def _alibi_slopes():
    return 2.0 ** (-8.0 * jnp.arange(1, N_HEADS + 1, dtype=F32) / N_HEADS)


def _attn_prompt(q, k, v, sinks, batch, seq):
    n = q.shape[0]
    nb = seq // ATTN_BLOCK
    smem = pl.BlockSpec(memory_space=pltpu.SMEM)
    cur = lambda w: pl.BlockSpec((ATTN_BLOCK, w), lambda b, i: (b * nb + i, 0))
    prv = lambda w: pl.BlockSpec((ATTN_BLOCK, w), lambda b, i: (b * nb + jnp.maximum(i - 1, 0), 0))
    return pl.pallas_call(
        _attn_prompt_kernel,
        grid=(batch, nb),
        in_specs=[smem, smem, cur(Q_WIDTH), prv(KV_WIDTH), cur(KV_WIDTH), prv(KV_WIDTH), cur(KV_WIDTH)],
        out_specs=cur(Q_WIDTH),
        out_shape=jax.ShapeDtypeStruct((n, Q_WIDTH), BF16),
        compiler_params=_cparams(("parallel", "arbitrary"), VMEM_LIMIT),
        name="attn_prompt",
    )(_alibi_slopes(), sinks.astype(F32), q, k, k, v, v)


def _attn_sample_kernel(slopes_ref, sinks_ref, q_ref, kn_ref, vn_ref, ck_ref, cv_ref,
                        o_ref, ko_ref, vo_ref):
    nseq = ck_ref.shape[0]
    t = SUBLANES
    keys = WINDOW + t
    qi = lax.broadcasted_iota(jnp.int32, (t, keys), 0)
    kj = lax.broadcasted_iota(jnp.int32, (t, keys), 1)
    dist = (WINDOW + qi) - kj
    valid = (dist >= 0) & (dist <= WINDOW)
    distf = dist.astype(F32)

    def one_seq(b, carry):
        kn = kn_ref[b]
        vn = vn_ref[b]
        ck = ck_ref[b]
        cv = cv_ref[b]
        kall = jnp.concatenate([ck, kn], axis=0)
        vall = jnp.concatenate([cv, vn], axis=0)
        ko_ref[b] = kall[t:]
        vo_ref[b] = vall[t:]
        kb = kall.astype(BF16)
        vb = vall.astype(BF16)
        o_ref[b] = _attend_heads(q_ref[b], kb, vb, slopes_ref, sinks_ref, distf, valid).astype(o_ref.dtype)
        return carry

    lax.fori_loop(0, nseq, one_seq, 0)


def _attn_sample(q, k_new, v_new, cache_k, cache_v, sinks, seqs_per_step=8):
    nb = cache_k.shape[0]
    t = SUBLANES
    sb = seqs_per_step
    smem = pl.BlockSpec(memory_space=pltpu.SMEM)
    blk = lambda r, w: pl.BlockSpec((sb, r, w), lambda i: (i, 0, 0))
    ck = cache_k.reshape(nb, WINDOW, KV_WIDTH)
    cv = cache_v.reshape(nb, WINDOW, KV_WIDTH)
    o, ko, vo = pl.pallas_call(
        _attn_sample_kernel,
        grid=(nb // sb,),
        in_specs=[smem, smem, blk(t, Q_WIDTH), blk(t, KV_WIDTH), blk(t, KV_WIDTH),
                  blk(WINDOW, KV_WIDTH), blk(WINDOW, KV_WIDTH)],
        out_specs=[blk(t, Q_WIDTH), blk(WINDOW, KV_WIDTH), blk(WINDOW, KV_WIDTH)],
        out_shape=[jax.ShapeDtypeStruct((nb, t, Q_WIDTH), BF16),
                   jax.ShapeDtypeStruct((nb, WINDOW, KV_WIDTH), F32),
                   jax.ShapeDtypeStruct((nb, WINDOW, KV_WIDTH), F32)],
        compiler_params=_cparams(("parallel",), VMEM_LIMIT),
        name="attn_sample",
    )(_alibi_slopes(), sinks.astype(F32), q.reshape(nb, t, Q_WIDTH), k_new.reshape(nb, t, KV_WIDTH),
      v_new.reshape(nb, t, KV_WIDTH), ck, cv)
    shape5 = (nb, WINDOW, N_KV_HEADS, HEAD_DIM)
    return o.reshape(nb * t, Q_WIDTH), ko.reshape(shape5), vo.reshape(shape5)


def _post_kernel(x_ref, lru_ref, attn_ref, sga_ref, sgb_ref, wl_ref, wa_ref, wo_ref, g2_ref,
                 hres_ref, xn2_ref, xn3_ref):
    merged = sga_ref[...] * _dot(lru_ref[...], wl_ref[...]) + sgb_ref[...] * _dot(attn_ref[...], wa_ref[...])
    hres = x_ref[...] + _dot(merged.astype(BF16), wo_ref[...])
    hres_ref[...] = hres
    y = hres * lax.rsqrt(jnp.mean(hres * hres, axis=-1, keepdims=True) + EPS)
    xn2 = y * g2_ref[...]
    xn2_ref[...] = xn2.astype(BF16)
    for s in range(ROW_CHUNKS):
        xn3_ref[:, s, :] = xn2[:, s * LANES:(s + 1) * LANES]


def _post(x, lru_out, attn, sga, sgb, wl, wa, wo, norm2_g, tm=256):
    n = x.shape[0]
    row = pl.BlockSpec((tm, D_MODEL), lambda i: (i, 0))
    row3 = pl.BlockSpec((tm, ROW_CHUNKS, LANES), lambda i: (i, 0, 0))
    full = lambda a: pl.BlockSpec(a.shape, lambda i: (0,) * a.ndim)
    g2 = norm2_g.reshape(1, D_MODEL)
    return pl.pallas_call(
        _post_kernel,
        grid=(n // tm,),
        in_specs=[row] * 5 + [full(wl), full(wa), full(wo), full(g2)],
        out_specs=[row, row, row3],
        out_shape=[jax.ShapeDtypeStruct((n, D_MODEL), F32), jax.ShapeDtypeStruct((n, D_MODEL), BF16),
                   jax.ShapeDtypeStruct((n, ROW_CHUNKS, LANES), F32)],
        compiler_params=_cparams(("parallel",), VMEM_LIMIT),
        name="post",
    )(x, lru_out, attn, sga, sgb, wl, wa, wo, g2)


def _topk_rows(s, k, pos):
    vals, sels = [], []
    for _ in range(k):
        m = jnp.max(s, axis=0, keepdims=True)
        sel = jnp.min(jnp.where(s == m, pos, float(1 << 20)), axis=0, keepdims=True)
        vals.append(m)
        sels.append(sel)
        s = jnp.where(pos == sel, -jnp.inf, s)
    return jnp.concatenate(vals, axis=0), jnp.concatenate(sels, axis=0)


def _pick_rows(table, which):
    out = jnp.zeros(which.shape, table.dtype)
    for a in range(table.shape[0]):
        out = out + jnp.where(which == float(a), table[a:a + 1, :], 0.0)
    return out


def _pair_candidates(s1, s2):
    tokens = s1.shape[1]
    row = lax.broadcasted_iota(jnp.int32, (SUBLANES, tokens), 0)
    rowf = row.astype(F32)
    neg = -jnp.inf
    lo1, hi1 = s1[:SUBLANES], s1[SUBLANES:]
    lo2, hi2 = s2[:SUBLANES], s2[SUBLANES:]
    k = float(PEER_TOPK)
    pieces = [
        (s1[0:1] + lo2, rowf),
        (s1[0:1] + hi2, rowf + 8.0),
        (s1[1:2] + lo2, rowf + k),
        (jnp.where(row < 5, s1[2:3] + lo2, neg), rowf + 2 * k),
        (jnp.where(row < 4, s1[3:4] + lo2, neg), rowf + 3 * k),
        (jnp.where(row >= 4, lo1 + s2[0:1], neg), rowf * k),
        (hi1 + s2[0:1], (rowf + 8.0) * k),
        (jnp.where(row >= 4, lo1 + s2[1:2], neg), rowf * k + 1.0),
        (jnp.where(row == 4, lo1 + s2[2:3], neg), rowf * k + 2.0),
    ]
    return (jnp.concatenate([p[0] for p in pieces], axis=0),
            jnp.concatenate([p[1] for p in pieces], axis=0))


def _route_kernel(xn_ref, wq_ref, keys_ref, idx_ref, gate_ref):
    qr = _dot(xn_ref[...], wq_ref[...]).astype(BF16)
    tokens = qr.shape[0]
    key_pos = lax.broadcasted_iota(jnp.int32, (N_KEYS, tokens), 0).astype(F32)
    for h in range(PEER_HEADS):
        tops = []
        for p in range(2):
            c = h * 2 + p
            st = _dot_nt(keys_ref[c], qr[:, c * KEY_DIM:(c + 1) * KEY_DIM])
            tops.append(_topk_rows(st, PEER_TOPK, key_pos))
        (s1, i1), (s2, i2) = tops
        cand, cand_pos = _pair_candidates(s1, s2)
        best, pos = _topk_rows(cand, PEER_TOPK, cand_pos)
        rank1 = jnp.floor(pos * (1.0 / PEER_TOPK))
        rank2 = pos - rank1 * PEER_TOPK
        expert = _pick_rows(i1, rank1) * N_KEYS + _pick_rows(i2, rank2)
        idx_ref[h * PEER_TOPK:(h + 1) * PEER_TOPK, :] = expert.astype(jnp.int32)
        ex = jnp.exp(best - best[0:1, :])
        gate_ref[h * PEER_TOPK:(h + 1) * PEER_TOPK, :] = ex / jnp.sum(ex, axis=0, keepdims=True)


def _route(xn2, wq_bf16, keys_bf16, tm=128):
    n = xn2.shape[0]
    full = lambda a: pl.BlockSpec(a.shape, lambda i: (0,) * a.ndim)
    col = pl.BlockSpec((N_ACTIVE, tm), lambda i: (0, i))
    return pl.pallas_call(
        _route_kernel,
        grid=(n // tm,),
        in_specs=[pl.BlockSpec((tm, D_MODEL), lambda i: (i, 0)), full(wq_bf16), full(keys_bf16)],
        out_specs=[col, col],
        out_shape=[jax.ShapeDtypeStruct((N_ACTIVE, n), jnp.int32), jax.ShapeDtypeStruct((N_ACTIVE, n), F32)],
        compiler_params=_cparams(("parallel",), VMEM_LIMIT),
        name="route",
    )(xn2, wq_bf16, keys_bf16)


PACK_ROWS = ROW_CHUNKS // 2
TILE_WORD_ROWS = N_ACTIVE * PACK_ROWS
TILE_ROWS = N_ACTIVE * ROW_CHUNKS


def _pack_table(w):
    wb = w.astype(BF16).reshape(N_EXPERTS, PACK_ROWS, 2, LANES)
    wb = jnp.transpose(wb, (0, 1, 3, 2))
    return lax.bitcast_convert_type(wb, jnp.int32).reshape(N_EXPERTS * PACK_ROWS, LANES)


def _load_table_once(tab_hbm, tab_vmem, sem):
    @pl.when(pl.program_id(0) == 0)
    def _():
        cp = pltpu.make_async_copy(tab_hbm, tab_vmem, sem)
        cp.start()
        cp.wait()


def _gather_tile(idx_ref, t, tab_vmem, tile_ref):
    row_ref = idx_ref.at[t]
    for j in range(N_ACTIVE):
        e4 = pl.multiple_of(row_ref[j], PACK_ROWS)
        tile_ref[j * PACK_ROWS:(j + 1) * PACK_ROWS, :] = tab_vmem[pl.ds(e4, PACK_ROWS), :]


def _chunk_diag_mask():
    sub = lax.broadcasted_iota(jnp.int32, (ROW_CHUNKS, TILE_ROWS), 0)
    lane = lax.broadcasted_iota(jnp.int32, (ROW_CHUNKS, TILE_ROWS), 1)
    return (lane % ROW_CHUNKS) == sub


def _token_pipeline(tb, idx_ref, tab_vmem, tile_a, tile_b, compute):
    _gather_tile(idx_ref, 0, tab_vmem, tile_a)
    _gather_tile(idx_ref, 1, tab_vmem, tile_b)

    def two_tokens(i, carry):
        t = 2 * i
        compute(t, tile_a)
        _gather_tile(idx_ref, jnp.minimum(t + 2, tb - 1), tab_vmem, tile_a)
        compute(t + 1, tile_b)
        _gather_tile(idx_ref, jnp.minimum(t + 3, tb - 1), tab_vmem, tile_b)
        return carry

    lax.fori_loop(0, tb // 2, two_tokens, 0)


def _peer_u_kernel(idx_ref, xn_ref, gate_ref, tab_hbm, sc_ref, tab_vmem, tile_a, tile_b, part_ref, sem):
    tb = xn_ref.shape[0]
    _load_table_once(tab_hbm, tab_vmem, sem)
    mask = _chunk_diag_mask()

    def compute(t, tile_ref):
        rows = pltpu.bitcast(tile_ref[...], BF16)
        xw = xn_ref[t].astype(BF16)
        r = _dot_nt(xw, rows)
        part_ref[t] = jnp.where(mask, r, 0.0)

    _token_pipeline(tb, idx_ref, tab_vmem, tile_a, tile_b, compute)
    part = part_ref[...].reshape(tb * ROW_CHUNKS, TILE_ROWS)
    per_chunk = _dot_exact01(part, _seg_matrix(TILE_ROWS, ROW_CHUNKS, N_ACTIVE))
    act = _dot_exact01_left(_seg_matrix_t(tb, tb * ROW_CHUNKS, ROW_CHUNKS), per_chunk)
    sc_ref[...] = jax.nn.gelu(act) * gate_ref[...]


def _dot_exact01_left(m01, x):
    hi = x.astype(BF16)
    r1 = x - hi.astype(F32)
    mid = r1.astype(BF16)
    lo = (r1 - mid.astype(F32)).astype(BF16)
    return _dot(m01, hi) + _dot(m01, mid) + _dot(m01, lo)


def _peer_v_kernel(idx_ref, sc_ref, hres_ref, tab_hbm, y_ref, tab_vmem, tile_a, tile_b, smat_ref, out3_ref,
                   sem):
    tb = sc_ref.shape[0]
    _load_table_once(tab_hbm, tab_vmem, sem)
    scb = sc_ref[...].astype(BF16)
    expanded = _dot(scb, _seg_matrix_t(N_ACTIVE, TILE_ROWS, ROW_CHUNKS)).astype(BF16)
    rep = _dot(_seg_matrix(tb * ROW_CHUNKS, ROW_CHUNKS, tb), expanded)
    sub = lax.broadcasted_iota(jnp.int32, (tb * ROW_CHUNKS, TILE_ROWS), 0) % ROW_CHUNKS
    lane = lax.broadcasted_iota(jnp.int32, (tb * ROW_CHUNKS, TILE_ROWS), 1) % ROW_CHUNKS
    smat_ref[...] = jnp.where(sub == lane, rep, 0.0).reshape(tb, ROW_CHUNKS, TILE_ROWS)

    def compute(t, tile_ref):
        rows = pltpu.bitcast(tile_ref[...], BF16)
        out3_ref[t] = _dot(smat_ref[t].astype(BF16), rows)

    _token_pipeline(tb, idx_ref, tab_vmem, tile_a, tile_b, compute)
    for s in range(ROW_CHUNKS):
        cols = slice(s * LANES, (s + 1) * LANES)
        y_ref[:, cols] = hres_ref[:, cols] + out3_ref[:, s, :]


def _peer_experts(idx, gate, xn3, hres, utab, vtab, tb=64):
    n = idx.shape[0]
    idx4 = idx * PACK_ROWS
    smem_blk = pl.BlockSpec((tb, N_ACTIVE), lambda i: (i, 0), memory_space=pltpu.SMEM)
    row = pl.BlockSpec((tb, N_ACTIVE), lambda i: (i, 0))
    wide = pl.BlockSpec((tb, D_MODEL), lambda i: (i, 0))
    row3 = pl.BlockSpec((tb, ROW_CHUNKS, LANES), lambda i: (i, 0, 0))
    anyspace = pl.BlockSpec(memory_space=pl.ANY)
    table_scratch = [pltpu.VMEM((N_EXPERTS * PACK_ROWS, LANES), jnp.int32),
                     pltpu.VMEM((TILE_WORD_ROWS, LANES), jnp.int32),
                     pltpu.VMEM((TILE_WORD_ROWS, LANES), jnp.int32),
                     pltpu.VMEM((tb, ROW_CHUNKS, TILE_ROWS), F32)]
    sc = pl.pallas_call(
        _peer_u_kernel,
        grid=(n // tb,),
        in_specs=[smem_blk, row3, row, anyspace],
        out_specs=row,
        out_shape=jax.ShapeDtypeStruct((n, N_ACTIVE), F32),
        scratch_shapes=table_scratch + [pltpu.SemaphoreType.DMA],
        compiler_params=_cparams(("arbitrary",), VMEM_LIMIT),
        name="peer_u",
    )(idx4, xn3, gate, utab)
    return pl.pallas_call(
        _peer_v_kernel,
        grid=(n // tb,),
        in_specs=[smem_blk, row, wide, anyspace],
        out_specs=wide,
        out_shape=jax.ShapeDtypeStruct((n, D_MODEL), F32),
        scratch_shapes=table_scratch + [pltpu.VMEM((tb, ROW_CHUNKS, LANES), F32), pltpu.SemaphoreType.DMA],
        compiler_params=_cparams(("arbitrary",), VMEM_LIMIT),
        name="peer_v",
    )(idx4, sc, hres, vtab)


def kernel(x_prompt, x_sample, cache_conv, state_lru, cache_k, cache_v, norm1_g, w_in, conv_w, conv_b,
           rg_w_a, rg_b_a, rg_w_x, rg_b_x, rg_lambda, q_norm_g, k_norm_g, attn_sinks, w_branch_lru,
           w_branch_attn, w_out, norm2_g, peer_w_query, peer_sub_keys, expert_u, expert_v):
    assert norm1_g.shape[0] == 1, "one layer"
    batch, seq, _ = x_prompt.shape
    dbatch, dseq, _ = x_sample.shape
    assert dseq == SUBLANES and seq % 256 == 0
    l = 0
    w_in_b = w_in[l].astype(BF16)
    lw = _lru_weights(conv_w[l], conv_b[l], rg_w_a[l], rg_b_a[l], rg_w_x[l], rg_b_x[l], rg_lambda[l])
    wl, wa, wo = (w[l].astype(BF16) for w in (w_branch_lru, w_branch_attn, w_out))
    wq = peer_w_query[l].astype(BF16)
    keys = peer_sub_keys[l].reshape(PEER_HEADS * 2, N_KEYS, KEY_DIM).astype(BF16)
    utab, vtab = _pack_table(expert_u[l]), _pack_table(expert_v[l])

    def tokens_after_mixers(x, lru_out, attn, sga, sgb):
        hres, xn2, xn3 = _post(x, lru_out, attn, sga, sgb, wl, wa, wo, norm2_g[l])
        idx_t, gate_t = _route(xn2, wq, keys)
        return _peer_experts(idx_t.T, gate_t.T, xn3, hres, utab, vtab)

    xp = x_prompt.reshape(batch * seq, D_MODEL)
    xr, ggr, q, k, v, sga, sgb = _inproj(xp, norm1_g[l], w_in_b, q_norm_g[l], k_norm_g[l])
    lru_p, hlast_p = _lru_prompt(xr, ggr, lw, batch, seq)
    attn_p = _attn_prompt(q, k, v, attn_sinks[l], batch, seq)
    y_p = tokens_after_mixers(xp, lru_p, attn_p, sga, sgb).reshape(batch, seq, D_MODEL)
    conv_p = xr.reshape(batch, seq, LRU_WIDTH)[:, seq - (CONV_WIDTH - 1):]
    kv5 = (batch, WINDOW, N_KV_HEADS, HEAD_DIM)
    k_p = k.reshape(batch, seq, KV_WIDTH)[:, seq - WINDOW:].reshape(kv5)
    v_p = v.reshape(batch, seq, KV_WIDTH)[:, seq - WINDOW:].reshape(kv5)

    xs = x_sample.reshape(dbatch * dseq, D_MODEL)
    xr, ggr, q, k, v, sga, sgb = _inproj(xs, norm1_g[l], w_in_b, q_norm_g[l], k_norm_g[l])
    lru_s, hlast_s = _lru_sample(xr, ggr, cache_conv[l], state_lru[l], lw)
    attn_s, k_s, v_s = _attn_sample(q, k, v, cache_k[l], cache_v[l], attn_sinks[l])
    y_s = tokens_after_mixers(xs, lru_s, attn_s, sga, sgb).reshape(dbatch, dseq, D_MODEL)
    conv_s = jnp.concatenate([cache_conv[l], xr.reshape(dbatch, dseq, LRU_WIDTH)],
                             axis=1)[:, -(CONV_WIDTH - 1):]

    st = lambda a: a[None]
    return (y_p, y_s, st(conv_p), st(hlast_p), st(k_p), st(v_p), st(conv_s), st(hlast_s), st(k_s), st(v_s))
```

```python
import functools

import jax
import jax.numpy as jnp
from jax import lax
from jax.experimental import pallas as pl
from jax.experimental.pallas import tpu as pltpu

D_MODEL = 1024
LRU_WIDTH = 1024
LRU_BLOCKS = 16
LRU_BLOCK = LRU_WIDTH // LRU_BLOCKS
CONV_WIDTH = 4
LRU_C = 8.0
N_HEADS = 16
N_KV_HEADS = 4
HEAD_DIM = 64
GROUP = N_HEADS // N_KV_HEADS
Q_WIDTH = N_HEADS * HEAD_DIM
KV_WIDTH = N_KV_HEADS * HEAD_DIM
WINDOW = 128
ATTN_BLOCK = 128
PAST_LEN = 16384
PEER_HEADS = 8
N_KEYS = 128
N_EXPERTS = N_KEYS * N_KEYS
KEY_DIM = 128
PEER_TOPK = 16
N_ACTIVE = PEER_HEADS * PEER_TOPK
IN_SPLITS = (LRU_WIDTH, LRU_WIDTH, Q_WIDTH, KV_WIDTH, KV_WIDTH, D_MODEL, D_MODEL)
IN_COLS = sum(IN_SPLITS)
EPS = 1e-6
NEG_INF = -1e30

LANES = 128
SUBLANES = 8
ROW_CHUNKS = D_MODEL // LANES
VMEM_LIMIT = 56 * 1024 * 1024

F32 = jnp.float32
BF16 = jnp.bfloat16


def _cparams(sem, vmem=None):
    return pltpu.CompilerParams(dimension_semantics=sem, vmem_limit_bytes=vmem)


def _dot(a, b):
    return jnp.dot(a, b, preferred_element_type=F32)


def _dot_nt(a, b):
    return lax.dot_general(a, b, (((1,), (1,)), ((), ())), preferred_element_type=F32)


def _dot_exact01(x, m01):
    hi = x.astype(BF16)
    r1 = x - hi.astype(F32)
    mid = r1.astype(BF16)
    lo = (r1 - mid.astype(F32)).astype(BF16)
    return _dot(hi, m01) + _dot(mid, m01) + _dot(lo, m01)


def _seg_matrix(width, seg, cols):
    c = lax.broadcasted_iota(jnp.int32, (width, cols), 0)
    h = lax.broadcasted_iota(jnp.int32, (width, cols), 1)
    return jnp.where(c // seg == h, 1.0, 0.0).astype(BF16)


def _seg_matrix_t(cols, width, seg):
    h = lax.broadcasted_iota(jnp.int32, (cols, width), 0)
    c = lax.broadcasted_iota(jnp.int32, (cols, width), 1)
    return jnp.where(c // seg == h, 1.0, 0.0).astype(BF16)


def _head_rmsnorm(t, gain_row, width):
    seg = _seg_matrix(width, HEAD_DIM, LANES)
    seg_t = _seg_matrix_t(LANES, width, HEAD_DIM)
    ssq = _dot_exact01(t * t, seg)
    inv = lax.rsqrt(ssq * (1.0 / HEAD_DIM) + EPS)
    inv_b = _dot_exact01(inv, seg_t)
    return t * inv_b * gain_row


def _inproj_kernel(x_ref, g1_ref, w_ref, qg_ref, kg_ref,
                   xr_ref, ggr_ref, q_ref, k_ref, v_ref, sga_ref, sgb_ref):
    x = x_ref[...]
    y = x * lax.rsqrt(jnp.mean(x * x, axis=-1, keepdims=True) + EPS)
    xn = (y * g1_ref[...]).astype(BF16)
    o = 0
    xr_ref[...] = _dot(xn, w_ref[:, o:o + LRU_WIDTH]); o += LRU_WIDTH
    ggr_ref[...] = jax.nn.gelu(_dot(xn, w_ref[:, o:o + LRU_WIDTH])); o += LRU_WIDTH
    q = _dot(xn, w_ref[:, o:o + Q_WIDTH]); o += Q_WIDTH
    q_ref[...] = (_head_rmsnorm(q, qg_ref[...], Q_WIDTH) * (HEAD_DIM ** -0.5)).astype(BF16)
    k = _dot(xn, w_ref[:, o:o + KV_WIDTH]); o += KV_WIDTH
    k_ref[...] = _head_rmsnorm(k, kg_ref[...], KV_WIDTH)
    v_ref[...] = _dot(xn, w_ref[:, o:o + KV_WIDTH]); o += KV_WIDTH
    sga_ref[...] = jax.nn.sigmoid(_dot(xn, w_ref[:, o:o + D_MODEL])); o += D_MODEL
    sgb_ref[...] = jax.nn.sigmoid(_dot(xn, w_ref[:, o:o + D_MODEL]))


def _inproj(x, norm1_g, w_in_bf16, q_norm_g, k_norm_g, tm=256):
    n = x.shape[0]
    assert n % tm == 0
    row = lambda w: pl.BlockSpec((tm, w), lambda i: (i, 0))
    full = lambda a: pl.BlockSpec(a.shape, lambda i: (0,) * a.ndim)
    g1 = norm1_g.reshape(1, D_MODEL)
    qg = jnp.tile(q_norm_g, N_HEADS).reshape(1, Q_WIDTH)
    kg = jnp.tile(k_norm_g, N_KV_HEADS).reshape(1, KV_WIDTH)
    outs = [(LRU_WIDTH, F32), (LRU_WIDTH, F32), (Q_WIDTH, BF16), (KV_WIDTH, F32), (KV_WIDTH, F32),
            (D_MODEL, F32), (D_MODEL, F32)]
    return pl.pallas_call(
        _inproj_kernel,
        grid=(n // tm,),
        in_specs=[row(D_MODEL), full(g1), full(w_in_bf16), full(qg), full(kg)],
        out_specs=[row(w) for w, _ in outs],
        out_shape=[jax.ShapeDtypeStruct((n, w), dt) for w, dt in outs],
        compiler_params=_cparams(("parallel",), VMEM_LIMIT),
        name="inproj",
    )(x, g1, w_in_bf16, qg, kg)


def _log_sigmoid(x):
    return jnp.minimum(x, 0.0) - jnp.log1p(jnp.exp(-jnp.abs(x)))


def _lru_gates(xc, wg_ref, ba_ref, bx_ref, lam_ref):
    g = _dot(xc.astype(BF16), wg_ref[...])
    r = jax.nn.sigmoid(g[:, :LRU_WIDTH] + ba_ref[...])
    i = jax.nn.sigmoid(g[:, LRU_WIDTH:] + bx_ref[...])
    log_a = LRU_C * r * _log_sigmoid(lam_ref[...])
    a = jnp.exp(log_a)
    b = jnp.sqrt(-jnp.tanh(log_a) * (a * a + 1.0)) * (i * xc)
    return a, b


def _shift_rows(x, s, tpos, fill):
    return jnp.where(tpos < s, fill, pltpu.roll(x, s, axis=0))


def _segment_scan(a, b, tpos, seg_len):
    s = 1
    while s < seg_len:
        a_prev = _shift_rows(a, s, tpos, 1.0)
        b_prev = _shift_rows(b, s, tpos, 0.0)
        b = a * b_prev + b
        a = a * a_prev
        s *= 2
    return a, b


def _lru_prompt_kernel(xr_ref, ggr_ref, cw_ref, cb_ref, wg_ref, ba_ref, bx_ref, lam_ref,
                       out_ref, hlast_ref, prev_ref, h_ref):
    tc = xr_ref.shape[0]

    @pl.when(pl.program_id(1) == 0)
    def _():
        prev_ref[...] = jnp.zeros_like(prev_ref)
        h_ref[...] = jnp.zeros_like(h_ref)

    xr = xr_ref[...]
    tpos = lax.broadcasted_iota(jnp.int32, (tc, LRU_WIDTH), 0)
    tpos8 = lax.broadcasted_iota(jnp.int32, (SUBLANES, LRU_WIDTH), 0)
    prev = prev_ref[...]
    xc = xr * cw_ref[CONV_WIDTH - 1:CONV_WIDTH, :] + cb_ref[...]
    for s in range(1, CONV_WIDTH):
        rolled = pltpu.roll(xr, s, axis=0)
        head = jnp.where(tpos8 < s, pltpu.roll(prev, s, axis=0), rolled[:SUBLANES])
        shifted = jnp.concatenate([head, rolled[SUBLANES:]], axis=0)
        xc = xc + shifted * cw_ref[CONV_WIDTH - 1 - s:CONV_WIDTH - s, :]
    a, b = _lru_gates(xc, wg_ref, ba_ref, bx_ref, lam_ref)
    a_cum, b_cum = _segment_scan(a, b, tpos, tc)
    h = a_cum * h_ref[0:1, :] + b_cum
    out_ref[...] = (h * ggr_ref[...]).astype(out_ref.dtype)
    last8 = h[tc - SUBLANES:]
    hlast_ref[...] = last8
    h_ref[...] = jnp.broadcast_to(last8[SUBLANES - 1:], h_ref.shape)
    prev_ref[...] = xr[tc - SUBLANES:]


def _lru_sample_kernel(xr_ref, ggr_ref, halo_ref, h0_ref, cw_ref, cb_ref, wg_ref, ba_ref, bx_ref,
                       lam_ref, out_ref, h_out_ref):
    rows = xr_ref.shape[0]
    xr = xr_ref[...]
    tpos = lax.broadcasted_iota(jnp.int32, (rows, LRU_WIDTH), 0) % SUBLANES
    xc = xr * cw_ref[CONV_WIDTH - 1:CONV_WIDTH, :] + cb_ref[...]
    for s in range(1, CONV_WIDTH):
        shifted = _shift_rows(xr, s, tpos, halo_ref[s - 1])
        xc = xc + shifted * cw_ref[CONV_WIDTH - 1 - s:CONV_WIDTH - s, :]
    a, b = _lru_gates(xc, wg_ref, ba_ref, bx_ref, lam_ref)
    b = b + a * h0_ref[...]
    _, h = _segment_scan(a, b, tpos, SUBLANES)
    out_ref[...] = (h * ggr_ref[...]).astype(out_ref.dtype)
    h_out_ref[...] = h


def _lru_weights(conv_w, conv_b, rg_w_a, rg_b_a, rg_w_x, rg_b_x, rg_lambda):
    eye = jnp.eye(LRU_BLOCKS, dtype=F32)
    bd = lambda w: jnp.einsum("njk,nm->njmk", w, eye).reshape(LRU_WIDTH, LRU_WIDTH)
    wg = jnp.concatenate([bd(rg_w_a), bd(rg_w_x)], axis=1).astype(BF16)
    r = lambda v: v.reshape(1, LRU_WIDTH)
    return conv_w, r(conv_b), wg, r(rg_b_a), r(rg_b_x), r(rg_lambda)


def _lru_prompt(xr, ggr, lw, batch, seq, tc=256):
    n = xr.shape[0]
    nt = seq // tc
    row = pl.BlockSpec((tc, LRU_WIDTH), lambda b, t: (b * nt + t, 0))
    full = lambda a: pl.BlockSpec(a.shape, lambda b, t: (0,) * a.ndim)
    out, hlast = pl.pallas_call(
        _lru_prompt_kernel,
        grid=(batch, nt),
        in_specs=[row, row] + [full(a) for a in lw],
        out_specs=[row, pl.BlockSpec((SUBLANES, LRU_WIDTH), lambda b, t: (b, 0))],
        out_shape=[jax.ShapeDtypeStruct((n, LRU_WIDTH), BF16),
                   jax.ShapeDtypeStruct((batch * SUBLANES, LRU_WIDTH), F32)],
        scratch_shapes=[pltpu.VMEM((SUBLANES, LRU_WIDTH), F32), pltpu.VMEM((SUBLANES, LRU_WIDTH), F32)],
        compiler_params=_cparams(("parallel", "arbitrary"), VMEM_LIMIT),
        name="lru_prompt",
    )(xr, ggr, *lw)
    return out, hlast.reshape(batch, SUBLANES, LRU_WIDTH)[:, SUBLANES - 1]


def _lru_sample(xr, ggr, conv_buf, h0, lw, rows_per_step=256):
    n = xr.shape[0]
    nb = n // SUBLANES
    halos = []
    for s in range(1, CONV_WIDTH):
        rows = conv_buf[:, CONV_WIDTH - 1 - s:, :]
        halos.append(jnp.concatenate(
            [rows, jnp.zeros((nb, SUBLANES - s, LRU_WIDTH), F32)], axis=1).reshape(n, LRU_WIDTH))
    halo = jnp.stack(halos)
    h0_rows = jnp.concatenate(
        [h0[:, None, :], jnp.zeros((nb, SUBLANES - 1, LRU_WIDTH), F32)], axis=1).reshape(n, LRU_WIDTH)
    r = rows_per_step
    row = pl.BlockSpec((r, LRU_WIDTH), lambda i: (i, 0))
    full = lambda a: pl.BlockSpec(a.shape, lambda i: (0,) * a.ndim)
    out, h = pl.pallas_call(
        _lru_sample_kernel,
        grid=(n // r,),
        in_specs=[row, row, pl.BlockSpec((CONV_WIDTH - 1, r, LRU_WIDTH), lambda i: (0, i, 0)), row]
        + [full(a) for a in lw],
        out_specs=[row, row],
        out_shape=[jax.ShapeDtypeStruct((n, LRU_WIDTH), BF16), jax.ShapeDtypeStruct((n, LRU_WIDTH), F32)],
        compiler_params=_cparams(("parallel",), VMEM_LIMIT),
        name="lru_sample",
    )(xr, ggr, halo, h0_rows, *lw)
    return out, h.reshape(nb, SUBLANES, LRU_WIDTH)[:, SUBLANES - 1]


def _attend_heads(q, k, v, slopes_ref, sinks_ref, distf, valid):
    heads = range(N_HEADS)
    col = lambda a, i: a[:, i * HEAD_DIM:(i + 1) * HEAD_DIM]
    ss = [jnp.where(valid, _dot_nt(col(q, h), col(k, h // GROUP)) - slopes_ref[h] * distf, NEG_INF)
          for h in heads]
    ms = [jnp.maximum(jnp.max(ss[h], axis=-1, keepdims=True), sinks_ref[h]) for h in heads]
    ps = [jnp.exp(ss[h] - ms[h]) for h in heads]
    dens = [jnp.sum(ps[h], axis=-1, keepdims=True) + jnp.exp(sinks_ref[h] - ms[h]) for h in heads]
    outs = [_dot(ps[h].astype(BF16), col(v, h // GROUP)) / dens[h] for h in heads]
    return jnp.concatenate(outs, axis=-1)


def _attn_prompt_kernel(slopes_ref, sinks_ref, q_ref, kp_ref, kc_ref, vp_ref, vc_ref, o_ref):
    blk = pl.program_id(1)
    tq = ATTN_BLOCK
    qi = lax.broadcasted_iota(jnp.int32, (tq, 2 * tq), 0)
    kj = lax.broadcasted_iota(jnp.int32, (tq, 2 * tq), 1)
    dist = (tq + qi) - kj
    valid = (dist >= 0) & (dist <= WINDOW) & ((kj >= tq) | (blk > 0))
    distf = dist.astype(F32)
    k = jnp.concatenate([kp_ref[...], kc_ref[...]], axis=0).astype(BF16)
    v = jnp.concatenate([vp_ref[...], vc_ref[...]], axis=0).astype(BF16)
    o_ref[...] = _attend_heads(q_ref[...], k, v, slopes_ref, sinks_ref, distf, valid).astype(o_ref.dtype)


def _alibi_slopes():
    return 2.0 ** (-8.0 * jnp.arange(1, N_HEADS + 1, dtype=F32) / N_HEADS)


def _attn_prompt(q, k, v, sinks, batch, seq):
    n = q.shape[0]
    nb = seq // ATTN_BLOCK
    smem = pl.BlockSpec(memory_space=pltpu.SMEM)
    cur = lambda w: pl.BlockSpec((ATTN_BLOCK, w), lambda b, i: (b * nb + i, 0))
    prv = lambda w: pl.BlockSpec((ATTN_BLOCK, w), lambda b, i: (b * nb + jnp.maximum(i - 1, 0), 0))
    return pl.pallas_call(
        _attn_prompt_kernel,
        grid=(batch, nb),
        in_specs=[smem, smem, cur(Q_WIDTH), prv(KV_WIDTH), cur(KV_WIDTH), prv(KV_WIDTH), cur(KV_WIDTH)],
        out_specs=cur(Q_WIDTH),
        out_shape=jax.ShapeDtypeStruct((n, Q_WIDTH), BF16),
        compiler_params=_cparams(("parallel", "arbitrary"), VMEM_LIMIT),
        name="attn_prompt",
    )(_alibi_slopes(), sinks.astype(F32), q, k, k, v, v)


def _attn_sample_kernel(slopes_ref, sinks_ref, q_ref, kn_ref, vn_ref, ck_ref, cv_ref,
                        o_ref, ko_ref, vo_ref):
    nseq = ck_ref.shape[0]
    t = SUBLANES
    keys = WINDOW + t
    qi = lax.broadcasted_iota(jnp.int32, (t, keys), 0)
    kj = lax.broadcasted_iota(jnp.int32, (t, keys), 1)
    dist = (WINDOW + qi) - kj
    valid = (dist >= 0) & (dist <= WINDOW)
    distf = dist.astype(F32)

    def one_seq(b, carry):
        kn = kn_ref[b]
        vn = vn_ref[b]
        ck = ck_ref[b]
        cv = cv_ref[b]
        kall = jnp.concatenate([ck, kn], axis=0)
        vall = jnp.concatenate([cv, vn], axis=0)
        ko_ref[b] = kall[t:]
        vo_ref[b] = vall[t:]
        kb = kall.astype(BF16)
        vb = vall.astype(BF16)
        o_ref[b] = _attend_heads(q_ref[b], kb, vb, slopes_ref, sinks_ref, distf, valid).astype(o_ref.dtype)
        return carry

    lax.fori_loop(0, nseq, one_seq, 0)


def _attn_sample(q, k_new, v_new, cache_k, cache_v, sinks, seqs_per_step=8):
    nb = cache_k.shape[0]
    t = SUBLANES
    sb = seqs_per_step
    smem = pl.BlockSpec(memory_space=pltpu.SMEM)
    blk = lambda r, w: pl.BlockSpec((sb, r, w), lambda i: (i, 0, 0))
    ck = cache_k.reshape(nb, WINDOW, KV_WIDTH)
    cv = cache_v.reshape(nb, WINDOW, KV_WIDTH)
    o, ko, vo = pl.pallas_call(
        _attn_sample_kernel,
        grid=(nb // sb,),
        in_specs=[smem, smem, blk(t, Q_WIDTH), blk(t, KV_WIDTH), blk(t, KV_WIDTH),
                  blk(WINDOW, KV_WIDTH), blk(WINDOW, KV_WIDTH)],
        out_specs=[blk(t, Q_WIDTH), blk(WINDOW, KV_WIDTH), blk(WINDOW, KV_WIDTH)],
        out_shape=[jax.ShapeDtypeStruct((nb, t, Q_WIDTH), BF16),
                   jax.ShapeDtypeStruct((nb, WINDOW, KV_WIDTH), F32),
                   jax.ShapeDtypeStruct((nb, WINDOW, KV_WIDTH), F32)],
        compiler_params=_cparams(("parallel",), VMEM_LIMIT),
        name="attn_sample",
    )(_alibi_slopes(), sinks.astype(F32), q.reshape(nb, t, Q_WIDTH), k_new.reshape(nb, t, KV_WIDTH),
      v_new.reshape(nb, t, KV_WIDTH), ck, cv)
    shape5 = (nb, WINDOW, N_KV_HEADS, HEAD_DIM)
    return o.reshape(nb * t, Q_WIDTH), ko.reshape(shape5), vo.reshape(shape5)


def _post_kernel(x_ref, lru_ref, attn_ref, sga_ref, sgb_ref, wl_ref, wa_ref, wo_ref, g2_ref,
                 hres_ref, xn2_ref, xn3_ref):
    merged = sga_ref[...] * _dot(lru_ref[...], wl_ref[...]) + sgb_ref[...] * _dot(attn_ref[...], wa_ref[...])
    hres = x_ref[...] + _dot(merged.astype(BF16), wo_ref[...])
    hres_ref[...] = hres
    y = hres * lax.rsqrt(jnp.mean(hres * hres, axis=-1, keepdims=True) + EPS)
    xn2 = y * g2_ref[...]
    xn2_ref[...] = xn2.astype(BF16)
    for s, c in enumerate(CHUNK_ORDER):
        xn3_ref[:, s, :] = xn2[:, c * LANES:(c + 1) * LANES]


def _post(x, lru_out, attn, sga, sgb, wl, wa, wo, norm2_g, tm=256):
    n = x.shape[0]
    row = pl.BlockSpec((tm, D_MODEL), lambda i: (i, 0))
    row3 = pl.BlockSpec((tm, ROW_CHUNKS, LANES), lambda i: (i, 0, 0))
    full = lambda a: pl.BlockSpec(a.shape, lambda i: (0,) * a.ndim)
    g2 = norm2_g.reshape(1, D_MODEL)
    return pl.pallas_call(
        _post_kernel,
        grid=(n // tm,),
        in_specs=[row] * 5 + [full(wl), full(wa), full(wo), full(g2)],
        out_specs=[row, row, row3],
        out_shape=[jax.ShapeDtypeStruct((n, D_MODEL), F32), jax.ShapeDtypeStruct((n, D_MODEL), BF16),
                   jax.ShapeDtypeStruct((n, ROW_CHUNKS, LANES), F32)],
        compiler_params=_cparams(("parallel",), VMEM_LIMIT),
        name="post",
    )(x, lru_out, attn, sga, sgb, wl, wa, wo, g2)


def _topk_rows(s, k, pos):
    vals, sels = [], []
    for _ in range(k):
        m = jnp.max(s, axis=0, keepdims=True)
        sel = jnp.min(jnp.where(s == m, pos, float(1 << 20)), axis=0, keepdims=True)
        vals.append(m)
        sels.append(sel)
        s = jnp.where(pos == sel, -jnp.inf, s)
    return jnp.concatenate(vals, axis=0), jnp.concatenate(sels, axis=0)


def _pick_rows(table, which):
    out = jnp.zeros(which.shape, table.dtype)
    for a in range(table.shape[0]):
        out = out + jnp.where(which == float(a), table[a:a + 1, :], 0.0)
    return out


def _pair_candidates(s1, s2):
    tokens = s1.shape[1]
    row = lax.broadcasted_iota(jnp.int32, (SUBLANES, tokens), 0)
    rowf = row.astype(F32)
    neg = -jnp.inf
    lo1, hi1 = s1[:SUBLANES], s1[SUBLANES:]
    lo2, hi2 = s2[:SUBLANES], s2[SUBLANES:]
    k = float(PEER_TOPK)
    pieces = [
        (s1[0:1] + lo2, rowf),
        (s1[0:1] + hi2, rowf + 8.0),
        (s1[1:2] + lo2, rowf + k),
        (jnp.where(row < 5, s1[2:3] + lo2, neg), rowf + 2 * k),
        (jnp.where(row < 4, s1[3:4] + lo2, neg), rowf + 3 * k),
        (jnp.where(row >= 4, lo1 + s2[0:1], neg), rowf * k),
        (hi1 + s2[0:1], (rowf + 8.0) * k),
        (jnp.where(row >= 4, lo1 + s2[1:2], neg), rowf * k + 1.0),
        (jnp.where(row == 4, lo1 + s2[2:3], neg), rowf * k + 2.0),
    ]
    return (jnp.concatenate([p[0] for p in pieces], axis=0),
            jnp.concatenate([p[1] for p in pieces], axis=0))


def _route_kernel(xn_ref, wq_ref, keys_ref, row_ref, gate_ref):
    qr = _dot(xn_ref[...], wq_ref[...]).astype(BF16)
    tokens = qr.shape[0]
    key_pos = lax.broadcasted_iota(jnp.int32, (N_KEYS, tokens), 0).astype(F32)
    rows_of, gates = [], []
    for h in range(PEER_HEADS):
        tops = []
        for p in range(2):
            c = h * 2 + p
            st = _dot_nt(keys_ref[c], qr[:, c * KEY_DIM:(c + 1) * KEY_DIM])
            tops.append(_topk_rows(st, PEER_TOPK, key_pos))
        (s1, i1), (s2, i2) = tops
        cand, cand_pos = _pair_candidates(s1, s2)
        best, pos = _topk_rows(cand, PEER_TOPK, cand_pos)
        rank1 = jnp.floor(pos * (1.0 / PEER_TOPK))
        rank2 = pos - rank1 * PEER_TOPK
        expert = _pick_rows(i1, rank1) * N_KEYS + _pick_rows(i2, rank2)
        rows_of.append(expert * float(PACK_ROWS))
        ex = jnp.exp(best - best[0:1, :])
        gates.append(ex / jnp.sum(ex, axis=0, keepdims=True))
    row_ref[...] = jnp.concatenate(rows_of, axis=0).astype(jnp.int32)
    gate_ref[...] = jnp.concatenate(gates, axis=0).T


def _route(xn2, wq_bf16, keys_bf16, tm=128):
    n = xn2.shape[0]
    full = lambda a: pl.BlockSpec(a.shape, lambda i: (0,) * a.ndim)
    col = pl.BlockSpec((tm, N_ACTIVE), lambda i: (i, 0))
    return pl.pallas_call(
        _route_kernel,
        grid=(n // tm,),
        in_specs=[pl.BlockSpec((tm, D_MODEL), lambda i: (i, 0)), full(wq_bf16), full(keys_bf16)],
        out_specs=[pl.BlockSpec((N_ACTIVE, tm), lambda i: (0, i)), col],
        out_shape=[jax.ShapeDtypeStruct((N_ACTIVE, n), jnp.int32), jax.ShapeDtypeStruct((n, N_ACTIVE), F32)],
        compiler_params=_cparams(("parallel",), VMEM_LIMIT),
        name="route",
    )(xn2, wq_bf16, keys_bf16)


PACK_ROWS = ROW_CHUNKS // 2
TILE_WORD_ROWS = N_ACTIVE * PACK_ROWS
TILE_ROWS = N_ACTIVE * ROW_CHUNKS


CHUNK_ORDER = tuple(c for a in range(PACK_ROWS) for c in (a, a + PACK_ROWS))


def _pack_table(w):
    wb = w.astype(BF16)
    half = D_MODEL // 2
    pairs = jnp.stack([wb[:, :half], wb[:, half:]], axis=-1)
    return lax.bitcast_convert_type(pairs, jnp.int32).reshape(N_EXPERTS * PACK_ROWS, LANES)


def _load_table_once(tab_hbm, tab_vmem, sem):
    @pl.when(pl.program_id(0) == 0)
    def _():
        cp = pltpu.make_async_copy(tab_hbm, tab_vmem, sem)
        cp.start()
        cp.wait()


def _chunk_diag_mask():
    sub = lax.broadcasted_iota(jnp.int32, (ROW_CHUNKS, TILE_ROWS), 0)
    lane = lax.broadcasted_iota(jnp.int32, (ROW_CHUNKS, TILE_ROWS), 1)
    return (lane % ROW_CHUNKS) == sub


def _token_pipeline(idx_hbm, idx_a, idx_b, sems, tab_vmem, tiles, compute):
    step, nsteps = pl.program_id(0), pl.num_programs(0)
    half = idx_a.shape[1]
    depth = len(tiles)

    def idx_copy(block, dst, sem):
        return pltpu.make_async_copy(idx_hbm.at[:, pl.ds(block * half, half)], dst, sem)

    def gather(idx_smem, t, tile_ref):
        for j in range(N_ACTIVE):
            e4 = pl.multiple_of(idx_smem.at[j][t], PACK_ROWS)
            tile_ref[j * PACK_ROWS:(j + 1) * PACK_ROWS, :] = tab_vmem[pl.ds(e4, PACK_ROWS), :]

    def token_groups(idx_smem, base):
        def body(i, carry):
            t = depth * i
            for d in range(depth):
                compute(base + t + d, tiles[d])
                gather(idx_smem, t + depth + d, tiles[d])
            return carry

        lax.fori_loop(0, half // depth - 1, body, 0)

    @pl.when(step == 0)
    def _():
        idx_copy(0, idx_a, sems.at[0]).start()
        idx_copy(1, idx_b, sems.at[1]).start()

    idx_copy(2 * step, idx_a, sems.at[0]).wait()
    for d in range(depth):
        gather(idx_a, d, tiles[d])
    token_groups(idx_a, 0)

    @pl.when(step + 1 < nsteps)
    def _():
        idx_copy(2 * step + 2, idx_a, sems.at[0]).start()

    idx_copy(2 * step + 1, idx_b, sems.at[1]).wait()
    for d in range(depth):
        compute(half - depth + d, tiles[d])
        gather(idx_b, d, tiles[d])
    token_groups(idx_b, half)
    for d in range(depth):
        compute(2 * half - depth + d, tiles[d])

    @pl.when(step + 1 < nsteps)
    def _():
        idx_copy(2 * step + 3, idx_b, sems.at[1]).start()


PIPELINE_DEPTH = 8


def _rows8(tb, body):
    def step(r, carry):
        body(pl.multiple_of(r * SUBLANES, SUBLANES))
        return carry

    lax.fori_loop(0, tb // SUBLANES, step, 0)


def _peer_u_kernel(idx_hbm, xn_ref, gate_ref, tab_hbm, sc_ref, tab_vmem, *scratch):
    tiles, (part3_ref, part_ref, idx_a, idx_b, sem, idx_sems) = scratch[:PIPELINE_DEPTH], scratch[PIPELINE_DEPTH:]
    tb = xn_ref.shape[0]
    _load_table_once(tab_hbm, tab_vmem, sem)
    mask = _chunk_diag_mask()

    def compute(t, tile_ref):
        rows = pltpu.bitcast(tile_ref[...], BF16)
        xw = xn_ref[t].astype(BF16)
        r = _dot_nt(xw, rows)
        part3_ref[t] = jnp.where(mask, r, 0.0)

    _token_pipeline(idx_hbm, idx_a, idx_b, idx_sems, tab_vmem, tiles, compute)

    def add_sublanes(r8):
        acc = part3_ref[pl.ds(r8, SUBLANES), 0, :]
        for s in range(1, ROW_CHUNKS):
            acc = acc + part3_ref[pl.ds(r8, SUBLANES), s, :]
        part_ref[pl.ds(r8, SUBLANES), :] = acc

    _rows8(tb, add_sublanes)
    act = _dot_exact01(part_ref[...], _seg_matrix(TILE_ROWS, ROW_CHUNKS, N_ACTIVE))
    sc = jax.nn.gelu(act) * gate_ref[...]
    for s in range(SUBLANES):
        sc_ref[:, s, :] = sc


def _peer_v_kernel(idx_hbm, sc_ref, hres_ref, tab_hbm, y_ref, tab_vmem, *scratch):
    tiles, (out3_ref, idx_a, idx_b, sem, idx_sems) = scratch[:PIPELINE_DEPTH], scratch[PIPELINE_DEPTH:]
    _load_table_once(tab_hbm, tab_vmem, sem)
    lane = lax.broadcasted_iota(jnp.int32, (SUBLANES, LANES), 1)
    sub = lax.broadcasted_iota(jnp.int32, (SUBLANES, LANES), 0)
    own_chunk = lane % ROW_CHUNKS == sub
    pairs_per_tile = LANES // ROW_CHUNKS

    def compute(t, tile_ref):
        rows = pltpu.bitcast(tile_ref[...], BF16)
        sc8 = sc_ref[t]
        smat = jnp.concatenate(
            [jnp.where(own_chunk, jnp.take_along_axis(sc8, c * pairs_per_tile + lane // ROW_CHUNKS, axis=1), 0.0)
             for c in range(TILE_ROWS // LANES)], axis=1)
        out3_ref[t] = _dot(smat.astype(BF16), rows)

    _token_pipeline(idx_hbm, idx_a, idx_b, idx_sems, tab_vmem, tiles, compute)
    for s, c in enumerate(CHUNK_ORDER):
        cols = slice(c * LANES, (c + 1) * LANES)
        y_ref[:, cols] = hres_ref[:, cols] + out3_ref[:, s, :]


def _peer_experts(idx4, gate, xn3, hres, utab, vtab, tb=2 * LANES):
    n = idx4.shape[1]
    half = tb // 2
    assert n % tb == 0 and half % LANES == 0 and half % PIPELINE_DEPTH == 0
    row = pl.BlockSpec((tb, N_ACTIVE), lambda i: (i, 0))
    wide = pl.BlockSpec((tb, D_MODEL), lambda i: (i, 0))
    row3 = pl.BlockSpec((tb, ROW_CHUNKS, LANES), lambda i: (i, 0, 0))
    anyspace = pl.BlockSpec(memory_space=pl.ANY)
    table_and_tiles = ([pltpu.VMEM((N_EXPERTS * PACK_ROWS, LANES), jnp.int32)]
                       + [pltpu.VMEM((TILE_WORD_ROWS, LANES), jnp.int32)] * PIPELINE_DEPTH)
    per_token = [pltpu.VMEM((tb, ROW_CHUNKS, TILE_ROWS), F32), pltpu.VMEM((tb, TILE_ROWS), F32)]
    idx_scratch = [pltpu.SMEM((N_ACTIVE, half), jnp.int32), pltpu.SMEM((N_ACTIVE, half), jnp.int32),
                   pltpu.SemaphoreType.DMA, pltpu.SemaphoreType.DMA((2,))]
    sc = pl.pallas_call(
        _peer_u_kernel,
        grid=(n // tb,),
        in_specs=[anyspace, row3, row, anyspace],
        out_specs=row3,
        out_shape=jax.ShapeDtypeStruct((n, SUBLANES, N_ACTIVE), F32),
        scratch_shapes=table_and_tiles + per_token + idx_scratch,
        compiler_params=_cparams(("arbitrary",), VMEM_LIMIT),
        name="peer_u",
    )(idx4, xn3, gate, utab)
    return pl.pallas_call(
        _peer_v_kernel,
        grid=(n // tb,),
        in_specs=[anyspace, row3, wide, anyspace],
        out_specs=wide,
        out_shape=jax.ShapeDtypeStruct((n, D_MODEL), F32),
        scratch_shapes=table_and_tiles + [pltpu.VMEM((tb, ROW_CHUNKS, LANES), F32)] + idx_scratch,
        compiler_params=_cparams(("arbitrary",), VMEM_LIMIT),
        name="peer_v",
    )(idx4, sc, hres, vtab)


def kernel(x_prompt, x_sample, cache_conv, state_lru, cache_k, cache_v, norm1_g, w_in, conv_w, conv_b,
           rg_w_a, rg_b_a, rg_w_x, rg_b_x, rg_lambda, q_norm_g, k_norm_g, attn_sinks, w_branch_lru,
           w_branch_attn, w_out, norm2_g, peer_w_query, peer_sub_keys, expert_u, expert_v):
    assert norm1_g.shape[0] == 1, "one layer"
    batch, seq, _ = x_prompt.shape
    dbatch, dseq, _ = x_sample.shape
    assert dseq == SUBLANES and seq % 256 == 0
    l = 0
    w_in_b = w_in[l].astype(BF16)
    lw = _lru_weights(conv_w[l], conv_b[l], rg_w_a[l], rg_b_a[l], rg_w_x[l], rg_b_x[l], rg_lambda[l])
    wl, wa, wo = (w[l].astype(BF16) for w in (w_branch_lru, w_branch_attn, w_out))
    wq = peer_w_query[l].astype(BF16)
    keys = peer_sub_keys[l].reshape(PEER_HEADS * 2, N_KEYS, KEY_DIM).astype(BF16)
    utab, vtab = _pack_table(expert_u[l]), _pack_table(expert_v[l])

    def tokens_after_mixers(x, lru_out, attn, sga, sgb):
        hres, xn2, xn3 = _post(x, lru_out, attn, sga, sgb, wl, wa, wo, norm2_g[l])
        idx4, gate = _route(xn2, wq, keys)
        return _peer_experts(idx4, gate, xn3, hres, utab, vtab)

    xp = x_prompt.reshape(batch * seq, D_MODEL)
    xr, ggr, q, k, v, sga, sgb = _inproj(xp, norm1_g[l], w_in_b, q_norm_g[l], k_norm_g[l])
    lru_p, hlast_p = _lru_prompt(xr, ggr, lw, batch, seq)
    attn_p = _attn_prompt(q, k, v, attn_sinks[l], batch, seq)
    y_p = tokens_after_mixers(xp, lru_p, attn_p, sga, sgb).reshape(batch, seq, D_MODEL)
    conv_p = xr.reshape(batch, seq, LRU_WIDTH)[:, seq - (CONV_WIDTH - 1):]
    kv5 = (batch, WINDOW, N_KV_HEADS, HEAD_DIM)
    k_p = k.reshape(batch, seq, KV_WIDTH)[:, seq - WINDOW:].reshape(kv5)
    v_p = v.reshape(batch, seq, KV_WIDTH)[:, seq - WINDOW:].reshape(kv5)

    xs = x_sample.reshape(dbatch * dseq, D_MODEL)
    xr, ggr, q, k, v, sga, sgb = _inproj(xs, norm1_g[l], w_in_b, q_norm_g[l], k_norm_g[l])
    lru_s, hlast_s = _lru_sample(xr, ggr, cache_conv[l], state_lru[l], lw)
    attn_s, k_s, v_s = _attn_sample(q, k, v, cache_k[l], cache_v[l], attn_sinks[l])
    y_s = tokens_after_mixers(xs, lru_s, attn_s, sga, sgb).reshape(dbatch, dseq, D_MODEL)
    conv_s = jnp.concatenate([cache_conv[l], xr.reshape(dbatch, dseq, LRU_WIDTH)],
                             axis=1)[:, -(CONV_WIDTH - 1):]

    st = lambda a: a[None]
    return (y_p, y_s, st(conv_p), st(hlast_p), st(k_p), st(v_p), st(conv_s), st(hlast_s), st(k_s), st(v_s))
```

```python
import functools

import jax
import jax.numpy as jnp
from jax import lax
from jax.experimental import pallas as pl
from jax.experimental.pallas import tpu as pltpu

D_MODEL = 1024
LRU_WIDTH = 1024
LRU_BLOCKS = 16
LRU_BLOCK = LRU_WIDTH // LRU_BLOCKS
CONV_WIDTH = 4
LRU_C = 8.0
N_HEADS = 16
N_KV_HEADS = 4
HEAD_DIM = 64
GROUP = N_HEADS // N_KV_HEADS
Q_WIDTH = N_HEADS * HEAD_DIM
KV_WIDTH = N_KV_HEADS * HEAD_DIM
WINDOW = 128
ATTN_BLOCK = 128
PAST_LEN = 16384
PEER_HEADS = 8
N_KEYS = 128
N_EXPERTS = N_KEYS * N_KEYS
KEY_DIM = 128
PEER_TOPK = 16
N_ACTIVE = PEER_HEADS * PEER_TOPK
IN_SPLITS = (LRU_WIDTH, LRU_WIDTH, Q_WIDTH, KV_WIDTH, KV_WIDTH, D_MODEL, D_MODEL)
IN_COLS = sum(IN_SPLITS)
EPS = 1e-6
NEG_INF = -1e30

LANES = 128
SUBLANES = 8
ROW_CHUNKS = D_MODEL // LANES
VMEM_LIMIT = 56 * 1024 * 1024

F32 = jnp.float32
BF16 = jnp.bfloat16


def _cparams(sem, vmem=None):
    return pltpu.CompilerParams(dimension_semantics=sem, vmem_limit_bytes=vmem)


def _dot(a, b):
    return jnp.dot(a, b, preferred_element_type=F32)


def _dot_nt(a, b):
    return lax.dot_general(a, b, (((1,), (1,)), ((), ())), preferred_element_type=F32)


def _dot_exact01(x, m01):
    hi = x.astype(BF16)
    r1 = x - hi.astype(F32)
    mid = r1.astype(BF16)
    lo = (r1 - mid.astype(F32)).astype(BF16)
    return _dot(hi, m01) + _dot(mid, m01) + _dot(lo, m01)


def _seg_matrix(width, seg, cols):
    c = lax.broadcasted_iota(jnp.int32, (width, cols), 0)
    h = lax.broadcasted_iota(jnp.int32, (width, cols), 1)
    return jnp.where(c // seg == h, 1.0, 0.0).astype(BF16)


def _seg_matrix_t(cols, width, seg):
    h = lax.broadcasted_iota(jnp.int32, (cols, width), 0)
    c = lax.broadcasted_iota(jnp.int32, (cols, width), 1)
    return jnp.where(c // seg == h, 1.0, 0.0).astype(BF16)


def _head_rmsnorm(t, gain_row, width):
    seg = _seg_matrix(width, HEAD_DIM, LANES)
    seg_t = _seg_matrix_t(LANES, width, HEAD_DIM)
    ssq = _dot_exact01(t * t, seg)
    inv = lax.rsqrt(ssq * (1.0 / HEAD_DIM) + EPS)
    inv_b = _dot_exact01(inv, seg_t)
    return t * inv_b * gain_row


def _inproj_kernel(x_ref, g1_ref, w_ref, qg_ref, kg_ref,
                   xr_ref, ggr_ref, q_ref, k_ref, v_ref, sga_ref, sgb_ref):
    x = x_ref[...]
    y = x * lax.rsqrt(jnp.mean(x * x, axis=-1, keepdims=True) + EPS)
    xn = (y * g1_ref[...]).astype(BF16)
    o = 0
    xr_ref[...] = _dot(xn, w_ref[:, o:o + LRU_WIDTH]); o += LRU_WIDTH
    ggr_ref[...] = jax.nn.gelu(_dot(xn, w_ref[:, o:o + LRU_WIDTH])); o += LRU_WIDTH
    q = _dot(xn, w_ref[:, o:o + Q_WIDTH]); o += Q_WIDTH
    q_ref[...] = (_head_rmsnorm(q, qg_ref[...], Q_WIDTH) * (HEAD_DIM ** -0.5)).astype(BF16)
    k = _dot(xn, w_ref[:, o:o + KV_WIDTH]); o += KV_WIDTH
    k_ref[...] = _head_rmsnorm(k, kg_ref[...], KV_WIDTH)
    v_ref[...] = _dot(xn, w_ref[:, o:o + KV_WIDTH]); o += KV_WIDTH
    sga_ref[...] = jax.nn.sigmoid(_dot(xn, w_ref[:, o:o + D_MODEL])); o += D_MODEL
    sgb_ref[...] = jax.nn.sigmoid(_dot(xn, w_ref[:, o:o + D_MODEL]))


def _inproj(x, norm1_g, w_in_bf16, q_norm_g, k_norm_g, tm=256):
    n = x.shape[0]
    assert n % tm == 0
    row = lambda w: pl.BlockSpec((tm, w), lambda i: (i, 0))
    full = lambda a: pl.BlockSpec(a.shape, lambda i: (0,) * a.ndim)
    g1 = norm1_g.reshape(1, D_MODEL)
    qg = jnp.tile(q_norm_g, N_HEADS).reshape(1, Q_WIDTH)
    kg = jnp.tile(k_norm_g, N_KV_HEADS).reshape(1, KV_WIDTH)
    outs = [(LRU_WIDTH, F32), (LRU_WIDTH, F32), (Q_WIDTH, BF16), (KV_WIDTH, F32), (KV_WIDTH, F32),
            (D_MODEL, F32), (D_MODEL, F32)]
    return pl.pallas_call(
        _inproj_kernel,
        grid=(n // tm,),
        in_specs=[row(D_MODEL), full(g1), full(w_in_bf16), full(qg), full(kg)],
        out_specs=[row(w) for w, _ in outs],
        out_shape=[jax.ShapeDtypeStruct((n, w), dt) for w, dt in outs],
        compiler_params=_cparams(("parallel",), VMEM_LIMIT),
        name="inproj",
    )(x, g1, w_in_bf16, qg, kg)


def _log_sigmoid(x):
    return jnp.minimum(x, 0.0) - jnp.log1p(jnp.exp(-jnp.abs(x)))


def _lru_gates(xc, wg_ref, ba_ref, bx_ref, lam_ref):
    g = _dot(xc.astype(BF16), wg_ref[...])
    r = jax.nn.sigmoid(g[:, :LRU_WIDTH] + ba_ref[...])
    i = jax.nn.sigmoid(g[:, LRU_WIDTH:] + bx_ref[...])
    log_a = LRU_C * r * _log_sigmoid(lam_ref[...])
    a = jnp.exp(log_a)
    b = jnp.sqrt(-jnp.tanh(log_a) * (a * a + 1.0)) * (i * xc)
    return a, b


def _shift_rows(x, s, tpos, fill):
    return jnp.where(tpos < s, fill, pltpu.roll(x, s, axis=0))


def _group_scan(a, b):
    rows, width = a.shape
    a = a.reshape(rows // SUBLANES, SUBLANES, width)
    b = b.reshape(rows // SUBLANES, SUBLANES, width)
    tpos = lax.broadcasted_iota(jnp.int32, a.shape, 1)
    s = 1
    while s < SUBLANES:
        a_prev = jnp.where(tpos < s, 1.0, pltpu.roll(a, s, axis=1))
        b_prev = jnp.where(tpos < s, 0.0, pltpu.roll(b, s, axis=1))
        b = a * b_prev + b
        a = a * a_prev
        s *= 2
    return a.reshape(rows, width), b.reshape(rows, width)


def _lru_prompt_kernel(xr_ref, ggr_ref, cw_ref, cb_ref, wg_ref, ba_ref, bx_ref, lam_ref,
                       out_ref, hlast_ref, prev_ref, h_ref):
    tc = xr_ref.shape[0]

    @pl.when(pl.program_id(1) == 0)
    def _():
        prev_ref[...] = jnp.zeros_like(prev_ref)
        h_ref[...] = jnp.zeros_like(h_ref)

    xr = xr_ref[...]
    tpos8 = lax.broadcasted_iota(jnp.int32, (SUBLANES, LRU_WIDTH), 0)
    prev = prev_ref[...]
    xc = xr * cw_ref[CONV_WIDTH - 1:CONV_WIDTH, :] + cb_ref[...]
    for s in range(1, CONV_WIDTH):
        rolled = pltpu.roll(xr, s, axis=0)
        head = jnp.where(tpos8 < s, pltpu.roll(prev, s, axis=0), rolled[:SUBLANES])
        shifted = jnp.concatenate([head, rolled[SUBLANES:]], axis=0)
        xc = xc + shifted * cw_ref[CONV_WIDTH - 1 - s:CONV_WIDTH - s, :]
    a, b = _lru_gates(xc, wg_ref, ba_ref, bx_ref, lam_ref)
    a8, b8 = _group_scan(a, b)
    carry = h_ref[0:1, :]
    groups = []
    for g in range(tc // SUBLANES):
        rows = slice(g * SUBLANES, (g + 1) * SUBLANES)
        groups.append(a8[rows] * carry + b8[rows])
        carry = groups[-1][SUBLANES - 1:]
    h = jnp.concatenate(groups, axis=0)
    out_ref[...] = (h * ggr_ref[...]).astype(out_ref.dtype)
    last8 = groups[-1]
    hlast_ref[...] = last8
    h_ref[...] = jnp.broadcast_to(last8[SUBLANES - 1:], h_ref.shape)
    prev_ref[...] = xr[tc - SUBLANES:]


def _lru_sample_kernel(xr_ref, ggr_ref, halo_ref, h0_ref, cw_ref, cb_ref, wg_ref, ba_ref, bx_ref,
                       lam_ref, out_ref, h_out_ref):
    rows = xr_ref.shape[0]
    xr = xr_ref[...]
    tpos = lax.broadcasted_iota(jnp.int32, (rows, LRU_WIDTH), 0) % SUBLANES
    xc = xr * cw_ref[CONV_WIDTH - 1:CONV_WIDTH, :] + cb_ref[...]
    for s in range(1, CONV_WIDTH):
        shifted = _shift_rows(xr, s, tpos, halo_ref[s - 1])
        xc = xc + shifted * cw_ref[CONV_WIDTH - 1 - s:CONV_WIDTH - s, :]
    a, b = _lru_gates(xc, wg_ref, ba_ref, bx_ref, lam_ref)
    b = b + a * h0_ref[...]
    _, h = _group_scan(a, b)
    out_ref[...] = (h * ggr_ref[...]).astype(out_ref.dtype)
    h_out_ref[...] = h


def _lru_weights(conv_w, conv_b, rg_w_a, rg_b_a, rg_w_x, rg_b_x, rg_lambda):
    eye = jnp.eye(LRU_BLOCKS, dtype=F32)
    bd = lambda w: jnp.einsum("njk,nm->njmk", w, eye).reshape(LRU_WIDTH, LRU_WIDTH)
    wg = jnp.concatenate([bd(rg_w_a), bd(rg_w_x)], axis=1).astype(BF16)
    r = lambda v: v.reshape(1, LRU_WIDTH)
    return conv_w, r(conv_b), wg, r(rg_b_a), r(rg_b_x), r(rg_lambda)


def _lru_prompt(xr, ggr, lw, batch, seq, tc=256):
    n = xr.shape[0]
    nt = seq // tc
    row = pl.BlockSpec((tc, LRU_WIDTH), lambda b, t: (b * nt + t, 0))
    full = lambda a: pl.BlockSpec(a.shape, lambda b, t: (0,) * a.ndim)
    out, hlast = pl.pallas_call(
        _lru_prompt_kernel,
        grid=(batch, nt),
        in_specs=[row, row] + [full(a) for a in lw],
        out_specs=[row, pl.BlockSpec((SUBLANES, LRU_WIDTH), lambda b, t: (b, 0))],
        out_shape=[jax.ShapeDtypeStruct((n, LRU_WIDTH), BF16),
                   jax.ShapeDtypeStruct((batch * SUBLANES, LRU_WIDTH), F32)],
        scratch_shapes=[pltpu.VMEM((SUBLANES, LRU_WIDTH), F32), pltpu.VMEM((SUBLANES, LRU_WIDTH), F32)],
        compiler_params=_cparams(("parallel", "arbitrary"), VMEM_LIMIT),
        name="lru_prompt",
    )(xr, ggr, *lw)
    return out, hlast.reshape(batch, SUBLANES, LRU_WIDTH)[:, SUBLANES - 1]


def _lru_sample(xr, ggr, conv_buf, h0, lw, rows_per_step=256):
    n = xr.shape[0]
    nb = n // SUBLANES
    halos = []
    for s in range(1, CONV_WIDTH):
        rows = conv_buf[:, CONV_WIDTH - 1 - s:, :]
        halos.append(jnp.concatenate(
            [rows, jnp.zeros((nb, SUBLANES - s, LRU_WIDTH), F32)], axis=1).reshape(n, LRU_WIDTH))
    halo = jnp.stack(halos)
    h0_rows = jnp.concatenate(
        [h0[:, None, :], jnp.zeros((nb, SUBLANES - 1, LRU_WIDTH), F32)], axis=1).reshape(n, LRU_WIDTH)
    r = rows_per_step
    row = pl.BlockSpec((r, LRU_WIDTH), lambda i: (i, 0))
    full = lambda a: pl.BlockSpec(a.shape, lambda i: (0,) * a.ndim)
    out, h = pl.pallas_call(
        _lru_sample_kernel,
        grid=(n // r,),
        in_specs=[row, row, pl.BlockSpec((CONV_WIDTH - 1, r, LRU_WIDTH), lambda i: (0, i, 0)), row]
        + [full(a) for a in lw],
        out_specs=[row, row],
        out_shape=[jax.ShapeDtypeStruct((n, LRU_WIDTH), BF16), jax.ShapeDtypeStruct((n, LRU_WIDTH), F32)],
        compiler_params=_cparams(("parallel",), VMEM_LIMIT),
        name="lru_sample",
    )(xr, ggr, halo, h0_rows, *lw)
    return out, h.reshape(nb, SUBLANES, LRU_WIDTH)[:, SUBLANES - 1]


def _attend_heads(q, k, v, slopes_ref, sinks_ref, distf, valid):
    heads = range(N_HEADS)
    col = lambda a, i: a[:, i * HEAD_DIM:(i + 1) * HEAD_DIM]
    ss = [jnp.where(valid, _dot_nt(col(q, h), col(k, h // GROUP)) - slopes_ref[h] * distf, NEG_INF)
          for h in heads]
    ms = [jnp.maximum(jnp.max(ss[h], axis=-1, keepdims=True), sinks_ref[h]) for h in heads]
    ps = [jnp.exp(ss[h] - ms[h]) for h in heads]
    dens = [jnp.sum(ps[h], axis=-1, keepdims=True) + jnp.exp(sinks_ref[h] - ms[h]) for h in heads]
    outs = [_dot(ps[h].astype(BF16), col(v, h // GROUP)) / dens[h] for h in heads]
    return jnp.concatenate(outs, axis=-1)


def _attn_prompt_kernel(slopes_ref, sinks_ref, q_ref, kp_ref, kc_ref, vp_ref, vc_ref, o_ref):
    blk = pl.program_id(1)
    tq = ATTN_BLOCK
    qi = lax.broadcasted_iota(jnp.int32, (tq, 2 * tq), 0)
    kj = lax.broadcasted_iota(jnp.int32, (tq, 2 * tq), 1)
    dist = (tq + qi) - kj
    valid = (dist >= 0) & (dist <= WINDOW) & ((kj >= tq) | (blk > 0))
    distf = dist.astype(F32)
    k = jnp.concatenate([kp_ref[...], kc_ref[...]], axis=0).astype(BF16)
    v = jnp.concatenate([vp_ref[...], vc_ref[...]], axis=0).astype(BF16)
    o_ref[...] = _attend_heads(q_ref[...], k, v, slopes_ref, sinks_ref, distf, valid).astype(o_ref.dtype)


def _alibi_slopes():
    return 2.0 ** (-8.0 * jnp.arange(1, N_HEADS + 1, dtype=F32) / N_HEADS)


def _attn_prompt(q, k, v, sinks, batch, seq):
    n = q.shape[0]
    nb = seq // ATTN_BLOCK
    smem = pl.BlockSpec(memory_space=pltpu.SMEM)
    cur = lambda w: pl.BlockSpec((ATTN_BLOCK, w), lambda b, i: (b * nb + i, 0))
    prv = lambda w: pl.BlockSpec((ATTN_BLOCK, w), lambda b, i: (b * nb + jnp.maximum(i - 1, 0), 0))
    return pl.pallas_call(
        _attn_prompt_kernel,
        grid=(batch, nb),
        in_specs=[smem, smem, cur(Q_WIDTH), prv(KV_WIDTH), cur(KV_WIDTH), prv(KV_WIDTH), cur(KV_WIDTH)],
        out_specs=cur(Q_WIDTH),
        out_shape=jax.ShapeDtypeStruct((n, Q_WIDTH), BF16),
        compiler_params=_cparams(("parallel", "arbitrary"), VMEM_LIMIT),
        name="attn_prompt",
    )(_alibi_slopes(), sinks.astype(F32), q, k, k, v, v)


def _attn_sample_kernel(slopes_ref, sinks_ref, q_ref, kn_ref, vn_ref, ck_ref, cv_ref,
                        o_ref, ko_ref, vo_ref):
    nseq = ck_ref.shape[0]
    t = SUBLANES
    keys = WINDOW + t
    qi = lax.broadcasted_iota(jnp.int32, (t, keys), 0)
    kj = lax.broadcasted_iota(jnp.int32, (t, keys), 1)
    dist = (WINDOW + qi) - kj
    valid = (dist >= 0) & (dist <= WINDOW)
    distf = dist.astype(F32)

    def one_seq(b, carry):
        kn = kn_ref[b]
        vn = vn_ref[b]
        ck = ck_ref[b]
        cv = cv_ref[b]
        kall = jnp.concatenate([ck, kn], axis=0)
        vall = jnp.concatenate([cv, vn], axis=0)
        ko_ref[b] = kall[t:]
        vo_ref[b] = vall[t:]
        kb = kall.astype(BF16)
        vb = vall.astype(BF16)
        o_ref[b] = _attend_heads(q_ref[b], kb, vb, slopes_ref, sinks_ref, distf, valid).astype(o_ref.dtype)
        return carry

    lax.fori_loop(0, nseq, one_seq, 0)


def _attn_sample(q, k_new, v_new, cache_k, cache_v, sinks, seqs_per_step=8):
    nb = cache_k.shape[0]
    t = SUBLANES
    sb = seqs_per_step
    smem = pl.BlockSpec(memory_space=pltpu.SMEM)
    blk = lambda r, w: pl.BlockSpec((sb, r, w), lambda i: (i, 0, 0))
    ck = cache_k.reshape(nb, WINDOW, KV_WIDTH)
    cv = cache_v.reshape(nb, WINDOW, KV_WIDTH)
    o, ko, vo = pl.pallas_call(
        _attn_sample_kernel,
        grid=(nb // sb,),
        in_specs=[smem, smem, blk(t, Q_WIDTH), blk(t, KV_WIDTH), blk(t, KV_WIDTH),
                  blk(WINDOW, KV_WIDTH), blk(WINDOW, KV_WIDTH)],
        out_specs=[blk(t, Q_WIDTH), blk(WINDOW, KV_WIDTH), blk(WINDOW, KV_WIDTH)],
        out_shape=[jax.ShapeDtypeStruct((nb, t, Q_WIDTH), BF16),
                   jax.ShapeDtypeStruct((nb, WINDOW, KV_WIDTH), F32),
                   jax.ShapeDtypeStruct((nb, WINDOW, KV_WIDTH), F32)],
        compiler_params=_cparams(("parallel",), VMEM_LIMIT),
        name="attn_sample",
    )(_alibi_slopes(), sinks.astype(F32), q.reshape(nb, t, Q_WIDTH), k_new.reshape(nb, t, KV_WIDTH),
      v_new.reshape(nb, t, KV_WIDTH), ck, cv)
    shape5 = (nb, WINDOW, N_KV_HEADS, HEAD_DIM)
    return o.reshape(nb * t, Q_WIDTH), ko.reshape(shape5), vo.reshape(shape5)


def _post_kernel(x_ref, lru_ref, attn_ref, sga_ref, sgb_ref, wl_ref, wa_ref, wo_ref, g2_ref,
                 hres_ref, xn2_ref, xn3_ref):
    merged = sga_ref[...] * _dot(lru_ref[...], wl_ref[...]) + sgb_ref[...] * _dot(attn_ref[...], wa_ref[...])
    hres = x_ref[...] + _dot(merged.astype(BF16), wo_ref[...])
    hres_ref[...] = hres
    y = hres * lax.rsqrt(jnp.mean(hres * hres, axis=-1, keepdims=True) + EPS)
    xn2 = y * g2_ref[...]
    xn2_ref[...] = xn2.astype(BF16)
    for s, c in enumerate(CHUNK_ORDER):
        xn3_ref[:, s, :] = xn2[:, c * LANES:(c + 1) * LANES]


def _post(x, lru_out, attn, sga, sgb, wl, wa, wo, norm2_g, tm=256):
    n = x.shape[0]
    row = pl.BlockSpec((tm, D_MODEL), lambda i: (i, 0))
    row3 = pl.BlockSpec((tm, ROW_CHUNKS, LANES), lambda i: (i, 0, 0))
    full = lambda a: pl.BlockSpec(a.shape, lambda i: (0,) * a.ndim)
    g2 = norm2_g.reshape(1, D_MODEL)
    return pl.pallas_call(
        _post_kernel,
        grid=(n // tm,),
        in_specs=[row] * 5 + [full(wl), full(wa), full(wo), full(g2)],
        out_specs=[row, row, row3],
        out_shape=[jax.ShapeDtypeStruct((n, D_MODEL), F32), jax.ShapeDtypeStruct((n, D_MODEL), BF16),
                   jax.ShapeDtypeStruct((n, ROW_CHUNKS, LANES), F32)],
        compiler_params=_cparams(("parallel",), VMEM_LIMIT),
        name="post",
    )(x, lru_out, attn, sga, sgb, wl, wa, wo, g2)


def _topk_rows(s, k, pos):
    vals, sels = [], []
    for _ in range(k):
        m = jnp.max(s, axis=0, keepdims=True)
        sel = jnp.min(jnp.where(s == m, pos, float(1 << 20)), axis=0, keepdims=True)
        vals.append(m)
        sels.append(sel)
        s = jnp.where(pos == sel, -jnp.inf, s)
    return jnp.concatenate(vals, axis=0), jnp.concatenate(sels, axis=0)


def _pick_rows(table, which):
    out = jnp.zeros(which.shape, table.dtype)
    for a in range(table.shape[0]):
        out = out + jnp.where(which == float(a), table[a:a + 1, :], 0.0)
    return out


def _pair_candidates(s1, s2):
    tokens = s1.shape[1]
    row = lax.broadcasted_iota(jnp.int32, (SUBLANES, tokens), 0)
    rowf = row.astype(F32)
    neg = -jnp.inf
    lo1, hi1 = s1[:SUBLANES], s1[SUBLANES:]
    lo2, hi2 = s2[:SUBLANES], s2[SUBLANES:]
    k = float(PEER_TOPK)
    pieces = [
        (s1[0:1] + lo2, rowf),
        (s1[0:1] + hi2, rowf + 8.0),
        (s1[1:2] + lo2, rowf + k),
        (jnp.where(row < 5, s1[2:3] + lo2, neg), rowf + 2 * k),
        (jnp.where(row < 4, s1[3:4] + lo2, neg), rowf + 3 * k),
        (jnp.where(row >= 4, lo1 + s2[0:1], neg), rowf * k),
        (hi1 + s2[0:1], (rowf + 8.0) * k),
        (jnp.where(row >= 4, lo1 + s2[1:2], neg), rowf * k + 1.0),
        (jnp.where(row == 4, lo1 + s2[2:3], neg), rowf * k + 2.0),
    ]
    return (jnp.concatenate([p[0] for p in pieces], axis=0),
            jnp.concatenate([p[1] for p in pieces], axis=0))


def _route_kernel(xn_ref, wq_ref, keys_ref, row_ref, gate_ref):
    qr = _dot(xn_ref[...], wq_ref[...]).astype(BF16)
    tokens = qr.shape[0]
    key_pos = lax.broadcasted_iota(jnp.int32, (N_KEYS, tokens), 0).astype(F32)
    rows_of, gates = [], []
    for h in range(PEER_HEADS):
        tops = []
        for p in range(2):
            c = h * 2 + p
            st = _dot_nt(keys_ref[c], qr[:, c * KEY_DIM:(c + 1) * KEY_DIM])
            tops.append(_topk_rows(st, PEER_TOPK, key_pos))
        (s1, i1), (s2, i2) = tops
        cand, cand_pos = _pair_candidates(s1, s2)
        best, pos = _topk_rows(cand, PEER_TOPK, cand_pos)
        rank1 = jnp.floor(pos * (1.0 / PEER_TOPK))
        rank2 = pos - rank1 * PEER_TOPK
        expert = _pick_rows(i1, rank1) * N_KEYS + _pick_rows(i2, rank2)
        rows_of.append(expert * float(PACK_ROWS))
        ex = jnp.exp(best - best[0:1, :])
        gates.append(ex / jnp.sum(ex, axis=0, keepdims=True))
    row_ref[...] = jnp.concatenate(rows_of, axis=0).astype(jnp.int32)
    gate_ref[...] = jnp.concatenate(gates, axis=0).T


def _route(xn2, wq_bf16, keys_bf16, tm=128):
    n = xn2.shape[0]
    full = lambda a: pl.BlockSpec(a.shape, lambda i: (0,) * a.ndim)
    col = pl.BlockSpec((tm, N_ACTIVE), lambda i: (i, 0))
    return pl.pallas_call(
        _route_kernel,
        grid=(n // tm,),
        in_specs=[pl.BlockSpec((tm, D_MODEL), lambda i: (i, 0)), full(wq_bf16), full(keys_bf16)],
        out_specs=[pl.BlockSpec((N_ACTIVE, tm), lambda i: (0, i)), col],
        out_shape=[jax.ShapeDtypeStruct((N_ACTIVE, n), jnp.int32), jax.ShapeDtypeStruct((n, N_ACTIVE), F32)],
        compiler_params=_cparams(("parallel",), VMEM_LIMIT),
        name="route",
    )(xn2, wq_bf16, keys_bf16)


PACK_ROWS = ROW_CHUNKS // 2
TILE_WORD_ROWS = N_ACTIVE * PACK_ROWS
TILE_ROWS = N_ACTIVE * ROW_CHUNKS


CHUNK_ORDER = tuple(c for a in range(PACK_ROWS) for c in (a, a + PACK_ROWS))


def _pack_table(w):
    wb = w.astype(BF16)
    half = D_MODEL // 2
    pairs = jnp.stack([wb[:, :half], wb[:, half:]], axis=-1)
    return lax.bitcast_convert_type(pairs, jnp.int32).reshape(N_EXPERTS * PACK_ROWS, LANES)


def _load_table_once(tab_hbm, tab_vmem, sem):
    @pl.when(pl.program_id(0) == 0)
    def _():
        cp = pltpu.make_async_copy(tab_hbm, tab_vmem, sem)
        cp.start()
        cp.wait()


def _token_pipeline(idx_hbm, idx_a, idx_b, sems, tab_vmem, tiles, compute):
    step, nsteps = pl.program_id(0), pl.num_programs(0)
    half = idx_a.shape[1]
    depth = len(tiles)

    def idx_copy(block, dst, sem):
        return pltpu.make_async_copy(idx_hbm.at[:, pl.ds(block * half, half)], dst, sem)

    def gather(idx_smem, t, tile_ref):
        for j in range(N_ACTIVE):
            e4 = pl.multiple_of(idx_smem.at[j][t], PACK_ROWS)
            tile_ref[j * PACK_ROWS:(j + 1) * PACK_ROWS, :] = tab_vmem[pl.ds(e4, PACK_ROWS), :]

    def token_groups(idx_smem, base):
        def body(i, carry):
            t = depth * i
            for d in range(depth):
                compute(base + t + d, tiles[d])
                gather(idx_smem, t + depth + d, tiles[d])
            return carry

        lax.fori_loop(0, half // depth - 1, body, 0)

    @pl.when(step == 0)
    def _():
        idx_copy(0, idx_a, sems.at[0]).start()
        idx_copy(1, idx_b, sems.at[1]).start()

    idx_copy(2 * step, idx_a, sems.at[0]).wait()
    for d in range(depth):
        gather(idx_a, d, tiles[d])
    token_groups(idx_a, 0)

    @pl.when(step + 1 < nsteps)
    def _():
        idx_copy(2 * step + 2, idx_a, sems.at[0]).start()

    idx_copy(2 * step + 1, idx_b, sems.at[1]).wait()
    for d in range(depth):
        compute(half - depth + d, tiles[d])
        gather(idx_b, d, tiles[d])
    token_groups(idx_b, half)
    for d in range(depth):
        compute(2 * half - depth + d, tiles[d])

    @pl.when(step + 1 < nsteps)
    def _():
        idx_copy(2 * step + 3, idx_b, sems.at[1]).start()


U_DEPTH = 16
V_DEPTH = 8

def _rows8(tb, body):
    def step(r, carry):
        body(pl.multiple_of(r * SUBLANES, SUBLANES))
        return carry

    lax.fori_loop(0, tb // SUBLANES, step, 0)


def _peer_u_kernel(idx_hbm, xn_ref, gate_ref, tab_hbm, sc_ref, tab_vmem, *scratch):
    tiles, (part3_ref, part_ref, idx_a, idx_b, sem, idx_sems) = scratch[:U_DEPTH], scratch[U_DEPTH:]
    tb = xn_ref.shape[0]
    _load_table_once(tab_hbm, tab_vmem, sem)
    sub = lax.broadcasted_iota(jnp.int32, (ROW_CHUNKS, TILE_ROWS), 0)
    lane = lax.broadcasted_iota(jnp.int32, (ROW_CHUNKS, TILE_ROWS), 1)
    own_chunk = (lane % ROW_CHUNKS) == sub

    def compute(t, tile_ref):
        rows = pltpu.bitcast(tile_ref[...], BF16)
        xw = xn_ref[t].astype(BF16)
        r = _dot_nt(xw, rows)
        part3_ref[t] = jnp.where(own_chunk, r, 0.0)

    _token_pipeline(idx_hbm, idx_a, idx_b, idx_sems, tab_vmem, tiles, compute)

    def add_sublanes(r8):
        acc = part3_ref[pl.ds(r8, SUBLANES), 0, :]
        for s in range(1, ROW_CHUNKS):
            acc = acc + part3_ref[pl.ds(r8, SUBLANES), s, :]
        part_ref[pl.ds(r8, SUBLANES), :] = acc

    _rows8(tb, add_sublanes)
    act = _dot_exact01(part_ref[...], _seg_matrix(TILE_ROWS, ROW_CHUNKS, N_ACTIVE))
    sc = jax.nn.gelu(act) * gate_ref[...]
    for s in range(SUBLANES):
        sc_ref[:, s, :] = sc


def _peer_v_kernel(idx_hbm, sc_ref, hres_ref, tab_hbm, y_ref, tab_vmem, *scratch):
    tiles, (out3_ref, idx_a, idx_b, sem, idx_sems) = scratch[:V_DEPTH], scratch[V_DEPTH:]
    _load_table_once(tab_hbm, tab_vmem, sem)
    lane = lax.broadcasted_iota(jnp.int32, (SUBLANES, LANES), 1)
    sub = lax.broadcasted_iota(jnp.int32, (SUBLANES, LANES), 0)
    own_chunk = lane % ROW_CHUNKS == sub
    pairs_per_tile = LANES // ROW_CHUNKS

    def compute(t, tile_ref):
        rows = pltpu.bitcast(tile_ref[...], BF16)
        sc8 = sc_ref[t]
        smat = jnp.concatenate(
            [jnp.where(own_chunk, jnp.take_along_axis(sc8, c * pairs_per_tile + lane // ROW_CHUNKS, axis=1), 0.0)
             for c in range(TILE_ROWS // LANES)], axis=1)
        out3_ref[t] = _dot(smat.astype(BF16), rows)

    _token_pipeline(idx_hbm, idx_a, idx_b, idx_sems, tab_vmem, tiles, compute)
    for s, c in enumerate(CHUNK_ORDER):
        cols = slice(c * LANES, (c + 1) * LANES)
        y_ref[:, cols] = hres_ref[:, cols] + out3_ref[:, s, :]


def _peer_experts(idx4, gate, xn3, hres, utab, vtab, tb=2 * LANES):
    n = idx4.shape[1]
    half = tb // 2
    assert n % tb == 0 and half % LANES == 0 and half % U_DEPTH == 0 and half % V_DEPTH == 0
    row = pl.BlockSpec((tb, N_ACTIVE), lambda i: (i, 0))
    wide = pl.BlockSpec((tb, D_MODEL), lambda i: (i, 0))
    row3 = pl.BlockSpec((tb, ROW_CHUNKS, LANES), lambda i: (i, 0, 0))
    anyspace = pl.BlockSpec(memory_space=pl.ANY)
    table_and_tiles = lambda depth: ([pltpu.VMEM((N_EXPERTS * PACK_ROWS, LANES), jnp.int32)]
                                     + [pltpu.VMEM((TILE_WORD_ROWS, LANES), jnp.int32)] * depth)
    per_token = [pltpu.VMEM((tb, ROW_CHUNKS, TILE_ROWS), F32), pltpu.VMEM((tb, TILE_ROWS), F32)]
    idx_scratch = [pltpu.SMEM((N_ACTIVE, half), jnp.int32), pltpu.SMEM((N_ACTIVE, half), jnp.int32),
                   pltpu.SemaphoreType.DMA, pltpu.SemaphoreType.DMA((2,))]
    sc = pl.pallas_call(
        _peer_u_kernel,
        grid=(n // tb,),
        in_specs=[anyspace, row3, row, anyspace],
        out_specs=row3,
        out_shape=jax.ShapeDtypeStruct((n, SUBLANES, N_ACTIVE), F32),
        scratch_shapes=table_and_tiles(U_DEPTH) + per_token + idx_scratch,
        compiler_params=_cparams(("arbitrary",), VMEM_LIMIT),
        name="peer_u",
    )(idx4, xn3, gate, utab)
    return pl.pallas_call(
        _peer_v_kernel,
        grid=(n // tb,),
        in_specs=[anyspace, row3, wide, anyspace],
        out_specs=wide,
        out_shape=jax.ShapeDtypeStruct((n, D_MODEL), F32),
        scratch_shapes=table_and_tiles(V_DEPTH) + [pltpu.VMEM((tb, ROW_CHUNKS, LANES), F32)] + idx_scratch,
        compiler_params=_cparams(("arbitrary",), VMEM_LIMIT),
        name="peer_v",
    )(idx4, sc, hres, vtab)


def kernel(x_prompt, x_sample, cache_conv, state_lru, cache_k, cache_v, norm1_g, w_in, conv_w, conv_b,
           rg_w_a, rg_b_a, rg_w_x, rg_b_x, rg_lambda, q_norm_g, k_norm_g, attn_sinks, w_branch_lru,
           w_branch_attn, w_out, norm2_g, peer_w_query, peer_sub_keys, expert_u, expert_v):
    assert norm1_g.shape[0] == 1, "one layer"
    batch, seq, _ = x_prompt.shape
    dbatch, dseq, _ = x_sample.shape
    assert dseq == SUBLANES and seq % 256 == 0
    l = 0
    w_in_b = w_in[l].astype(BF16)
    lw = _lru_weights(conv_w[l], conv_b[l], rg_w_a[l], rg_b_a[l], rg_w_x[l], rg_b_x[l], rg_lambda[l])
    wl, wa, wo = (w[l].astype(BF16) for w in (w_branch_lru, w_branch_attn, w_out))
    wq = peer_w_query[l].astype(BF16)
    keys = peer_sub_keys[l].reshape(PEER_HEADS * 2, N_KEYS, KEY_DIM).astype(BF16)
    utab, vtab = _pack_table(expert_u[l]), _pack_table(expert_v[l])

    def tokens_after_mixers(x, lru_out, attn, sga, sgb):
        hres, xn2, xn3 = _post(x, lru_out, attn, sga, sgb, wl, wa, wo, norm2_g[l])
        idx4, gate = _route(xn2, wq, keys)
        return _peer_experts(idx4, gate, xn3, hres, utab, vtab)

    xp = x_prompt.reshape(batch * seq, D_MODEL)
    xr, ggr, q, k, v, sga, sgb = _inproj(xp, norm1_g[l], w_in_b, q_norm_g[l], k_norm_g[l])
    lru_p, hlast_p = _lru_prompt(xr, ggr, lw, batch, seq)
    attn_p = _attn_prompt(q, k, v, attn_sinks[l], batch, seq)
    y_p = tokens_after_mixers(xp, lru_p, attn_p, sga, sgb).reshape(batch, seq, D_MODEL)
    conv_p = xr.reshape(batch, seq, LRU_WIDTH)[:, seq - (CONV_WIDTH - 1):]
    kv5 = (batch, WINDOW, N_KV_HEADS, HEAD_DIM)
    k_p = k.reshape(batch, seq, KV_WIDTH)[:, seq - WINDOW:].reshape(kv5)
    v_p = v.reshape(batch, seq, KV_WIDTH)[:, seq - WINDOW:].reshape(kv5)

    xs = x_sample.reshape(dbatch * dseq, D_MODEL)
    xr, ggr, q, k, v, sga, sgb = _inproj(xs, norm1_g[l], w_in_b, q_norm_g[l], k_norm_g[l])
    lru_s, hlast_s = _lru_sample(xr, ggr, cache_conv[l], state_lru[l], lw)
    attn_s, k_s, v_s = _attn_sample(q, k, v, cache_k[l], cache_v[l], attn_sinks[l])
    y_s = tokens_after_mixers(xs, lru_s, attn_s, sga, sgb).reshape(dbatch, dseq, D_MODEL)
    conv_s = jnp.concatenate([cache_conv[l], xr.reshape(dbatch, dseq, LRU_WIDTH)],
                             axis=1)[:, -(CONV_WIDTH - 1):]

    st = lambda a: a[None]
    return (y_p, y_s, st(conv_p), st(hlast_p), st(k_p), st(v_p), st(conv_s), st(hlast_s), st(k_s), st(v_s))
```

```python
import functools

import jax
import jax.numpy as jnp
from jax import lax
from jax.experimental import pallas as pl
from jax.experimental.pallas import tpu as pltpu

D_MODEL = 1024
LRU_WIDTH = 1024
LRU_BLOCKS = 16
LRU_BLOCK = LRU_WIDTH // LRU_BLOCKS
CONV_WIDTH = 4
LRU_C = 8.0
N_HEADS = 16
N_KV_HEADS = 4
HEAD_DIM = 64
GROUP = N_HEADS // N_KV_HEADS
Q_WIDTH = N_HEADS * HEAD_DIM
KV_WIDTH = N_KV_HEADS * HEAD_DIM
WINDOW = 128
ATTN_BLOCK = 128
PAST_LEN = 16384
PEER_HEADS = 8
N_KEYS = 128
N_EXPERTS = N_KEYS * N_KEYS
KEY_DIM = 128
PEER_TOPK = 16
N_ACTIVE = PEER_HEADS * PEER_TOPK
IN_SPLITS = (LRU_WIDTH, LRU_WIDTH, Q_WIDTH, KV_WIDTH, KV_WIDTH, D_MODEL, D_MODEL)
IN_COLS = sum(IN_SPLITS)
EPS = 1e-6
NEG_INF = -1e30

LANES = 128
SUBLANES = 8
ROW_CHUNKS = D_MODEL // LANES
VMEM_LIMIT = 56 * 1024 * 1024

F32 = jnp.float32
BF16 = jnp.bfloat16


def _cparams(sem, vmem=None):
    return pltpu.CompilerParams(dimension_semantics=sem, vmem_limit_bytes=vmem)


def _dot(a, b):
    return jnp.dot(a, b, preferred_element_type=F32)


def _dot_nt(a, b):
    return lax.dot_general(a, b, (((1,), (1,)), ((), ())), preferred_element_type=F32)


def _dot_exact01(x, m01):
    hi = x.astype(BF16)
    r1 = x - hi.astype(F32)
    mid = r1.astype(BF16)
    lo = (r1 - mid.astype(F32)).astype(BF16)
    return _dot(hi, m01) + _dot(mid, m01) + _dot(lo, m01)


def _seg_matrix(width, seg, cols):
    c = lax.broadcasted_iota(jnp.int32, (width, cols), 0)
    h = lax.broadcasted_iota(jnp.int32, (width, cols), 1)
    return jnp.where(c // seg == h, 1.0, 0.0).astype(BF16)


def _seg_matrix_t(cols, width, seg):
    h = lax.broadcasted_iota(jnp.int32, (cols, width), 0)
    c = lax.broadcasted_iota(jnp.int32, (cols, width), 1)
    return jnp.where(c // seg == h, 1.0, 0.0).astype(BF16)


def _head_rmsnorm(t, gain_row, width):
    seg = _seg_matrix(width, HEAD_DIM, LANES)
    seg_t = _seg_matrix_t(LANES, width, HEAD_DIM)
    ssq = _dot_exact01(t * t, seg)
    inv = lax.rsqrt(ssq * (1.0 / HEAD_DIM) + EPS)
    inv_b = _dot_exact01(inv, seg_t)
    return t * inv_b * gain_row


def _inproj_kernel(x_ref, g1_ref, w_ref, qg_ref, kg_ref,
                   xr_ref, ggr_ref, q_ref, k_ref, v_ref, sga_ref, sgb_ref):
    x = x_ref[...]
    y = x * lax.rsqrt(jnp.mean(x * x, axis=-1, keepdims=True) + EPS)
    xn = (y * g1_ref[...]).astype(BF16)
    o = 0
    xr_ref[...] = _dot(xn, w_ref[:, o:o + LRU_WIDTH]); o += LRU_WIDTH
    ggr_ref[...] = jax.nn.gelu(_dot(xn, w_ref[:, o:o + LRU_WIDTH])); o += LRU_WIDTH
    q = _dot(xn, w_ref[:, o:o + Q_WIDTH]); o += Q_WIDTH
    q_ref[...] = (_head_rmsnorm(q, qg_ref[...], Q_WIDTH) * (HEAD_DIM ** -0.5)).astype(BF16)
    k = _dot(xn, w_ref[:, o:o + KV_WIDTH]); o += KV_WIDTH
    k_ref[...] = _head_rmsnorm(k, kg_ref[...], KV_WIDTH)
    v_ref[...] = _dot(xn, w_ref[:, o:o + KV_WIDTH]); o += KV_WIDTH
    sga_ref[...] = jax.nn.sigmoid(_dot(xn, w_ref[:, o:o + D_MODEL])); o += D_MODEL
    sgb_ref[...] = jax.nn.sigmoid(_dot(xn, w_ref[:, o:o + D_MODEL]))


def _inproj(x, norm1_g, w_in_bf16, q_norm_g, k_norm_g, tm=256):
    n = x.shape[0]
    assert n % tm == 0
    row = lambda w: pl.BlockSpec((tm, w), lambda i: (i, 0))
    full = lambda a: pl.BlockSpec(a.shape, lambda i: (0,) * a.ndim)
    g1 = norm1_g.reshape(1, D_MODEL)
    qg = jnp.tile(q_norm_g, N_HEADS).reshape(1, Q_WIDTH)
    kg = jnp.tile(k_norm_g, N_KV_HEADS).reshape(1, KV_WIDTH)
    outs = [(LRU_WIDTH, F32), (LRU_WIDTH, F32), (Q_WIDTH, BF16), (KV_WIDTH, F32), (KV_WIDTH, F32),
            (D_MODEL, F32), (D_MODEL, F32)]
    return pl.pallas_call(
        _inproj_kernel,
        grid=(n // tm,),
        in_specs=[row(D_MODEL), full(g1), full(w_in_bf16), full(qg), full(kg)],
        out_specs=[row(w) for w, _ in outs],
        out_shape=[jax.ShapeDtypeStruct((n, w), dt) for w, dt in outs],
        compiler_params=_cparams(("parallel",), VMEM_LIMIT),
        name="inproj",
    )(x, g1, w_in_bf16, qg, kg)


def _log_sigmoid(x):
    return jnp.minimum(x, 0.0) - jnp.log1p(jnp.exp(-jnp.abs(x)))


def _lru_gates(xc, wg_ref, ba_ref, bx_ref, lam_ref):
    g = _dot(xc.astype(BF16), wg_ref[...])
    r = jax.nn.sigmoid(g[:, :LRU_WIDTH] + ba_ref[...])
    i = jax.nn.sigmoid(g[:, LRU_WIDTH:] + bx_ref[...])
    log_a = LRU_C * r * _log_sigmoid(lam_ref[...])
    a = jnp.exp(log_a)
    b = jnp.sqrt(-jnp.tanh(log_a) * (a * a + 1.0)) * (i * xc)
    return a, b


def _shift_rows(x, s, tpos, fill):
    return jnp.where(tpos < s, fill, pltpu.roll(x, s, axis=0))


def _group_scan(a, b):
    rows, width = a.shape
    a = a.reshape(rows // SUBLANES, SUBLANES, width)
    b = b.reshape(rows // SUBLANES, SUBLANES, width)
    tpos = lax.broadcasted_iota(jnp.int32, a.shape, 1)
    s = 1
    while s < SUBLANES:
        a_prev = jnp.where(tpos < s, 1.0, pltpu.roll(a, s, axis=1))
        b_prev = jnp.where(tpos < s, 0.0, pltpu.roll(b, s, axis=1))
        b = a * b_prev + b
        a = a * a_prev
        s *= 2
    return a.reshape(rows, width), b.reshape(rows, width)


def _lru_prompt_kernel(xr_ref, ggr_ref, cw_ref, cb_ref, wg_ref, ba_ref, bx_ref, lam_ref,
                       out_ref, hlast_ref, prev_ref, h_ref):
    tc = xr_ref.shape[0]

    @pl.when(pl.program_id(1) == 0)
    def _():
        prev_ref[...] = jnp.zeros_like(prev_ref)
        h_ref[...] = jnp.zeros_like(h_ref)

    xr = xr_ref[...]
    tpos8 = lax.broadcasted_iota(jnp.int32, (SUBLANES, LRU_WIDTH), 0)
    prev = prev_ref[...]
    xc = xr * cw_ref[CONV_WIDTH - 1:CONV_WIDTH, :] + cb_ref[...]
    for s in range(1, CONV_WIDTH):
        rolled = pltpu.roll(xr, s, axis=0)
        head = jnp.where(tpos8 < s, pltpu.roll(prev, s, axis=0), rolled[:SUBLANES])
        shifted = jnp.concatenate([head, rolled[SUBLANES:]], axis=0)
        xc = xc + shifted * cw_ref[CONV_WIDTH - 1 - s:CONV_WIDTH - s, :]
    a, b = _lru_gates(xc, wg_ref, ba_ref, bx_ref, lam_ref)
    a8, b8 = _group_scan(a, b)
    carry = h_ref[0:1, :]
    groups = []
    for g in range(tc // SUBLANES):
        rows = slice(g * SUBLANES, (g + 1) * SUBLANES)
        groups.append(a8[rows] * carry + b8[rows])
        carry = groups[-1][SUBLANES - 1:]
    h = jnp.concatenate(groups, axis=0)
    out_ref[...] = (h * ggr_ref[...]).astype(out_ref.dtype)
    last8 = groups[-1]
    hlast_ref[...] = last8
    h_ref[...] = jnp.broadcast_to(last8[SUBLANES - 1:], h_ref.shape)
    prev_ref[...] = xr[tc - SUBLANES:]


def _lru_sample_kernel(xr_ref, ggr_ref, halo_ref, h0_ref, cw_ref, cb_ref, wg_ref, ba_ref, bx_ref,
                       lam_ref, out_ref, h_out_ref):
    rows = xr_ref.shape[0]
    xr = xr_ref[...]
    tpos = lax.broadcasted_iota(jnp.int32, (rows, LRU_WIDTH), 0) % SUBLANES
    xc = xr * cw_ref[CONV_WIDTH - 1:CONV_WIDTH, :] + cb_ref[...]
    for s in range(1, CONV_WIDTH):
        shifted = _shift_rows(xr, s, tpos, halo_ref[s - 1])
        xc = xc + shifted * cw_ref[CONV_WIDTH - 1 - s:CONV_WIDTH - s, :]
    a, b = _lru_gates(xc, wg_ref, ba_ref, bx_ref, lam_ref)
    b = b + a * h0_ref[...]
    _, h = _group_scan(a, b)
    out_ref[...] = (h * ggr_ref[...]).astype(out_ref.dtype)
    h_out_ref[...] = h


def _lru_weights(conv_w, conv_b, rg_w_a, rg_b_a, rg_w_x, rg_b_x, rg_lambda):
    eye = jnp.eye(LRU_BLOCKS, dtype=F32)
    bd = lambda w: jnp.einsum("njk,nm->njmk", w, eye).reshape(LRU_WIDTH, LRU_WIDTH)
    wg = jnp.concatenate([bd(rg_w_a), bd(rg_w_x)], axis=1).astype(BF16)
    r = lambda v: v.reshape(1, LRU_WIDTH)
    return conv_w, r(conv_b), wg, r(rg_b_a), r(rg_b_x), r(rg_lambda)


def _lru_prompt(xr, ggr, lw, batch, seq, tc=256):
    n = xr.shape[0]
    nt = seq // tc
    row = pl.BlockSpec((tc, LRU_WIDTH), lambda b, t: (b * nt + t, 0))
    full = lambda a: pl.BlockSpec(a.shape, lambda b, t: (0,) * a.ndim)
    out, hlast = pl.pallas_call(
        _lru_prompt_kernel,
        grid=(batch, nt),
        in_specs=[row, row] + [full(a) for a in lw],
        out_specs=[row, pl.BlockSpec((SUBLANES, LRU_WIDTH), lambda b, t: (b, 0))],
        out_shape=[jax.ShapeDtypeStruct((n, LRU_WIDTH), BF16),
                   jax.ShapeDtypeStruct((batch * SUBLANES, LRU_WIDTH), F32)],
        scratch_shapes=[pltpu.VMEM((SUBLANES, LRU_WIDTH), F32), pltpu.VMEM((SUBLANES, LRU_WIDTH), F32)],
        compiler_params=_cparams(("parallel", "arbitrary"), VMEM_LIMIT),
        name="lru_prompt",
    )(xr, ggr, *lw)
    return out, hlast.reshape(batch, SUBLANES, LRU_WIDTH)[:, SUBLANES - 1]


def _lru_sample(xr, ggr, conv_buf, h0, lw, rows_per_step=256):
    n = xr.shape[0]
    nb = n // SUBLANES
    halos = []
    for s in range(1, CONV_WIDTH):
        rows = conv_buf[:, CONV_WIDTH - 1 - s:, :]
        halos.append(jnp.concatenate(
            [rows, jnp.zeros((nb, SUBLANES - s, LRU_WIDTH), F32)], axis=1).reshape(n, LRU_WIDTH))
    halo = jnp.stack(halos)
    h0_rows = jnp.concatenate(
        [h0[:, None, :], jnp.zeros((nb, SUBLANES - 1, LRU_WIDTH), F32)], axis=1).reshape(n, LRU_WIDTH)
    r = rows_per_step
    row = pl.BlockSpec((r, LRU_WIDTH), lambda i: (i, 0))
    full = lambda a: pl.BlockSpec(a.shape, lambda i: (0,) * a.ndim)
    out, h = pl.pallas_call(
        _lru_sample_kernel,
        grid=(n // r,),
        in_specs=[row, row, pl.BlockSpec((CONV_WIDTH - 1, r, LRU_WIDTH), lambda i: (0, i, 0)), row]
        + [full(a) for a in lw],
        out_specs=[row, row],
        out_shape=[jax.ShapeDtypeStruct((n, LRU_WIDTH), BF16), jax.ShapeDtypeStruct((n, LRU_WIDTH), F32)],
        compiler_params=_cparams(("parallel",), VMEM_LIMIT),
        name="lru_sample",
    )(xr, ggr, halo, h0_rows, *lw)
    return out, h.reshape(nb, SUBLANES, LRU_WIDTH)[:, SUBLANES - 1]


def _attend_heads(q, k, v, slopes_ref, sinks_ref, distf, valid):
    heads = range(N_HEADS)
    col = lambda a, i: a[:, i * HEAD_DIM:(i + 1) * HEAD_DIM]
    ss = [jnp.where(valid, _dot_nt(col(q, h), col(k, h // GROUP)) - slopes_ref[h] * distf, NEG_INF)
          for h in heads]
    ms = [jnp.maximum(jnp.max(ss[h], axis=-1, keepdims=True), sinks_ref[h]) for h in heads]
    ps = [jnp.exp(ss[h] - ms[h]) for h in heads]
    dens = [jnp.sum(ps[h], axis=-1, keepdims=True) + jnp.exp(sinks_ref[h] - ms[h]) for h in heads]
    outs = [_dot(ps[h].astype(BF16), col(v, h // GROUP)) / dens[h] for h in heads]
    return jnp.concatenate(outs, axis=-1)


def _attn_prompt_kernel(slopes_ref, sinks_ref, q_ref, kp_ref, kc_ref, vp_ref, vc_ref, o_ref):
    blk = pl.program_id(1)
    tq = ATTN_BLOCK
    qi = lax.broadcasted_iota(jnp.int32, (tq, 2 * tq), 0)
    kj = lax.broadcasted_iota(jnp.int32, (tq, 2 * tq), 1)
    dist = (tq + qi) - kj
    valid = (dist >= 0) & (dist <= WINDOW) & ((kj >= tq) | (blk > 0))
    distf = dist.astype(F32)
    k = jnp.concatenate([kp_ref[...], kc_ref[...]], axis=0).astype(BF16)
    v = jnp.concatenate([vp_ref[...], vc_ref[...]], axis=0).astype(BF16)
    o_ref[...] = _attend_heads(q_ref[...], k, v, slopes_ref, sinks_ref, distf, valid).astype(o_ref.dtype)


def _alibi_slopes():
    return 2.0 ** (-8.0 * jnp.arange(1, N_HEADS + 1, dtype=F32) / N_HEADS)


def _attn_prompt(q, k, v, sinks, batch, seq):
    n = q.shape[0]
    nb = seq // ATTN_BLOCK
    smem = pl.BlockSpec(memory_space=pltpu.SMEM)
    cur = lambda w: pl.BlockSpec((ATTN_BLOCK, w), lambda b, i: (b * nb + i, 0))
    prv = lambda w: pl.BlockSpec((ATTN_BLOCK, w), lambda b, i: (b * nb + jnp.maximum(i - 1, 0), 0))
    return pl.pallas_call(
        _attn_prompt_kernel,
        grid=(batch, nb),
        in_specs=[smem, smem, cur(Q_WIDTH), prv(KV_WIDTH), cur(KV_WIDTH), prv(KV_WIDTH), cur(KV_WIDTH)],
        out_specs=cur(Q_WIDTH),
        out_shape=jax.ShapeDtypeStruct((n, Q_WIDTH), BF16),
        compiler_params=_cparams(("parallel", "arbitrary"), VMEM_LIMIT),
        name="attn_prompt",
    )(_alibi_slopes(), sinks.astype(F32), q, k, k, v, v)


def _attn_sample_kernel(slopes_ref, sinks_ref, q_ref, kn_ref, vn_ref, ck_ref, cv_ref,
                        o_ref, ko_ref, vo_ref):
    nseq = ck_ref.shape[0]
    t = SUBLANES
    keys = WINDOW + t
    qi = lax.broadcasted_iota(jnp.int32, (t, keys), 0)
    kj = lax.broadcasted_iota(jnp.int32, (t, keys), 1)
    dist = (WINDOW + qi) - kj
    valid = (dist >= 0) & (dist <= WINDOW)
    distf = dist.astype(F32)

    def one_seq(b, carry):
        kn = kn_ref[b]
        vn = vn_ref[b]
        ck = ck_ref[b]
        cv = cv_ref[b]
        kall = jnp.concatenate([ck, kn], axis=0)
        vall = jnp.concatenate([cv, vn], axis=0)
        ko_ref[b] = kall[t:]
        vo_ref[b] = vall[t:]
        kb = kall.astype(BF16)
        vb = vall.astype(BF16)
        o_ref[b] = _attend_heads(q_ref[b], kb, vb, slopes_ref, sinks_ref, distf, valid).astype(o_ref.dtype)
        return carry

    lax.fori_loop(0, nseq, one_seq, 0)


def _attn_sample(q, k_new, v_new, cache_k, cache_v, sinks, seqs_per_step=8):
    nb = cache_k.shape[0]
    t = SUBLANES
    sb = seqs_per_step
    smem = pl.BlockSpec(memory_space=pltpu.SMEM)
    blk = lambda r, w: pl.BlockSpec((sb, r, w), lambda i: (i, 0, 0))
    ck = cache_k.reshape(nb, WINDOW, KV_WIDTH)
    cv = cache_v.reshape(nb, WINDOW, KV_WIDTH)
    o, ko, vo = pl.pallas_call(
        _attn_sample_kernel,
        grid=(nb // sb,),
        in_specs=[smem, smem, blk(t, Q_WIDTH), blk(t, KV_WIDTH), blk(t, KV_WIDTH),
                  blk(WINDOW, KV_WIDTH), blk(WINDOW, KV_WIDTH)],
        out_specs=[blk(t, Q_WIDTH), blk(WINDOW, KV_WIDTH), blk(WINDOW, KV_WIDTH)],
        out_shape=[jax.ShapeDtypeStruct((nb, t, Q_WIDTH), BF16),
                   jax.ShapeDtypeStruct((nb, WINDOW, KV_WIDTH), F32),
                   jax.ShapeDtypeStruct((nb, WINDOW, KV_WIDTH), F32)],
        compiler_params=_cparams(("parallel",), VMEM_LIMIT),
        name="attn_sample",
    )(_alibi_slopes(), sinks.astype(F32), q.reshape(nb, t, Q_WIDTH), k_new.reshape(nb, t, KV_WIDTH),
      v_new.reshape(nb, t, KV_WIDTH), ck, cv)
    shape5 = (nb, WINDOW, N_KV_HEADS, HEAD_DIM)
    return o.reshape(nb * t, Q_WIDTH), ko.reshape(shape5), vo.reshape(shape5)


def _post_kernel(x_ref, lru_ref, attn_ref, sga_ref, sgb_ref, wl_ref, wa_ref, wo_ref, g2_ref,
                 hres_ref, xn2_ref, xn3_ref):
    merged = sga_ref[...] * _dot(lru_ref[...], wl_ref[...]) + sgb_ref[...] * _dot(attn_ref[...], wa_ref[...])
    hres = x_ref[...] + _dot(merged.astype(BF16), wo_ref[...])
    hres_ref[...] = hres
    y = hres * lax.rsqrt(jnp.mean(hres * hres, axis=-1, keepdims=True) + EPS)
    xn2 = y * g2_ref[...]
    xn2_ref[...] = xn2.astype(BF16)
    for s, c in enumerate(CHUNK_ORDER):
        xn3_ref[:, s, :] = xn2[:, c * LANES:(c + 1) * LANES]


def _post(x, lru_out, attn, sga, sgb, wl, wa, wo, norm2_g, tm=256):
    n = x.shape[0]
    row = pl.BlockSpec((tm, D_MODEL), lambda i: (i, 0))
    row3 = pl.BlockSpec((tm, ROW_CHUNKS, LANES), lambda i: (i, 0, 0))
    full = lambda a: pl.BlockSpec(a.shape, lambda i: (0,) * a.ndim)
    g2 = norm2_g.reshape(1, D_MODEL)
    return pl.pallas_call(
        _post_kernel,
        grid=(n // tm,),
        in_specs=[row] * 5 + [full(wl), full(wa), full(wo), full(g2)],
        out_specs=[row, row, row3],
        out_shape=[jax.ShapeDtypeStruct((n, D_MODEL), F32), jax.ShapeDtypeStruct((n, D_MODEL), BF16),
                   jax.ShapeDtypeStruct((n, ROW_CHUNKS, LANES), F32)],
        compiler_params=_cparams(("parallel",), VMEM_LIMIT),
        name="post",
    )(x, lru_out, attn, sga, sgb, wl, wa, wo, g2)


def _topk_rows(s, k, pos):
    vals, sels = [], []
    for _ in range(k):
        m = jnp.max(s, axis=0, keepdims=True)
        sel = jnp.min(jnp.where(s == m, pos, float(1 << 20)), axis=0, keepdims=True)
        vals.append(m)
        sels.append(sel)
        s = jnp.where(pos == sel, -jnp.inf, s)
    return jnp.concatenate(vals, axis=0), jnp.concatenate(sels, axis=0)


def _pick_rows(table, which):
    out = jnp.zeros(which.shape, table.dtype)
    for a in range(table.shape[0]):
        out = out + jnp.where(which == float(a), table[a:a + 1, :], 0.0)
    return out


def _pair_candidates(s1, s2):
    tokens = s1.shape[1]
    row = lax.broadcasted_iota(jnp.int32, (SUBLANES, tokens), 0)
    rowf = row.astype(F32)
    neg = -jnp.inf
    lo1, hi1 = s1[:SUBLANES], s1[SUBLANES:]
    lo2, hi2 = s2[:SUBLANES], s2[SUBLANES:]
    k = float(PEER_TOPK)
    pieces = [
        (s1[0:1] + lo2, rowf),
        (s1[0:1] + hi2, rowf + 8.0),
        (s1[1:2] + lo2, rowf + k),
        (jnp.where(row < 5, s1[2:3] + lo2, neg), rowf + 2 * k),
        (jnp.where(row < 4, s1[3:4] + lo2, neg), rowf + 3 * k),
        (jnp.where(row >= 4, lo1 + s2[0:1], neg), rowf * k),
        (hi1 + s2[0:1], (rowf + 8.0) * k),
        (jnp.where(row >= 4, lo1 + s2[1:2], neg), rowf * k + 1.0),
        (jnp.where(row == 4, lo1 + s2[2:3], neg), rowf * k + 2.0),
    ]
    return (jnp.concatenate([p[0] for p in pieces], axis=0),
            jnp.concatenate([p[1] for p in pieces], axis=0))


def _route_kernel(xn_ref, wq_ref, keys_ref, row_ref, gate_ref):
    qr = _dot(xn_ref[...], wq_ref[...]).astype(BF16)
    tokens = qr.shape[0]
    key_pos = lax.broadcasted_iota(jnp.int32, (N_KEYS, tokens), 0).astype(F32)
    rows_of, gates = [], []
    for h in range(PEER_HEADS):
        tops = []
        for p in range(2):
            c = h * 2 + p
            st = _dot_nt(keys_ref[c], qr[:, c * KEY_DIM:(c + 1) * KEY_DIM])
            tops.append(_topk_rows(st, PEER_TOPK, key_pos))
        (s1, i1), (s2, i2) = tops
        cand, cand_pos = _pair_candidates(s1, s2)
        best, pos = _topk_rows(cand, PEER_TOPK, cand_pos)
        rank1 = jnp.floor(pos * (1.0 / PEER_TOPK))
        rank2 = pos - rank1 * PEER_TOPK
        expert = _pick_rows(i1, rank1) * N_KEYS + _pick_rows(i2, rank2)
        rows_of.append(expert * float(PACK_ROWS))
        ex = jnp.exp(best - best[0:1, :])
        gates.append(ex / jnp.sum(ex, axis=0, keepdims=True))
    row_ref[...] = jnp.concatenate(rows_of, axis=0).astype(jnp.int32)
    gate_ref[...] = jnp.concatenate(gates, axis=0).T


def _route(xn2, wq_bf16, keys_bf16, tm=128):
    n = xn2.shape[0]
    full = lambda a: pl.BlockSpec(a.shape, lambda i: (0,) * a.ndim)
    col = pl.BlockSpec((tm, N_ACTIVE), lambda i: (i, 0))
    return pl.pallas_call(
        _route_kernel,
        grid=(n // tm,),
        in_specs=[pl.BlockSpec((tm, D_MODEL), lambda i: (i, 0)), full(wq_bf16), full(keys_bf16)],
        out_specs=[pl.BlockSpec((N_ACTIVE, tm), lambda i: (0, i)), col],
        out_shape=[jax.ShapeDtypeStruct((N_ACTIVE, n), jnp.int32), jax.ShapeDtypeStruct((n, N_ACTIVE), F32)],
        compiler_params=_cparams(("parallel",), VMEM_LIMIT),
        name="route",
    )(xn2, wq_bf16, keys_bf16)


PACK_ROWS = ROW_CHUNKS // 2
TILE_WORD_ROWS = N_ACTIVE * PACK_ROWS
TILE_ROWS = N_ACTIVE * ROW_CHUNKS


CHUNK_ORDER = tuple(c for a in range(PACK_ROWS) for c in (a, a + PACK_ROWS))


def _pack_table(w):
    wb = w.astype(BF16)
    half = D_MODEL // 2
    pairs = jnp.stack([wb[:, :half], wb[:, half:]], axis=-1)
    return lax.bitcast_convert_type(pairs, jnp.int32).reshape(N_EXPERTS * PACK_ROWS, LANES)


def _load_table_once(tab_hbm, tab_vmem, sem):
    @pl.when(pl.program_id(0) == 0)
    def _():
        cp = pltpu.make_async_copy(tab_hbm, tab_vmem, sem)
        cp.start()
        cp.wait()


def _token_pipeline(idx_hbm, idx_a, idx_b, sems, tab_vmem, tiles, compute):
    step, nsteps = pl.program_id(0), pl.num_programs(0)
    half = idx_a.shape[1]
    depth = len(tiles)

    def idx_copy(block, dst, sem):
        return pltpu.make_async_copy(idx_hbm.at[:, pl.ds(block * half, half)], dst, sem)

    def gather(idx_smem, t, tile_ref):
        for j in range(N_ACTIVE):
            e4 = pl.multiple_of(idx_smem.at[j][t], PACK_ROWS)
            tile_ref[j * PACK_ROWS:(j + 1) * PACK_ROWS, :] = tab_vmem[pl.ds(e4, PACK_ROWS), :]

    def token_groups(idx_smem, base):
        def body(i, carry):
            t = depth * i
            for d in range(depth):
                compute(base + t + d, tiles[d])
                gather(idx_smem, t + depth + d, tiles[d])
            return carry

        lax.fori_loop(0, half // depth - 1, body, 0)

    last = step + 1 == nsteps

    @pl.when(step == 0)
    def _():
        idx_copy(0, idx_a, sems.at[0]).start()
        idx_copy(1, idx_b, sems.at[1]).start()
        idx_copy(0, idx_a, sems.at[0]).wait()
        for d in range(depth):
            gather(idx_a, d, tiles[d])

    token_groups(idx_a, 0)

    @pl.when(jnp.logical_not(last))
    def _():
        idx_copy(2 * step + 2, idx_a, sems.at[0]).start()

    idx_copy(2 * step + 1, idx_b, sems.at[1]).wait()
    for d in range(depth):
        compute(half - depth + d, tiles[d])
        gather(idx_b, d, tiles[d])
    token_groups(idx_b, half)

    @pl.when(last)
    def _():
        for d in range(depth):
            compute(2 * half - depth + d, tiles[d])

    @pl.when(jnp.logical_not(last))
    def _():
        idx_copy(2 * step + 3, idx_b, sems.at[1]).start()
        idx_copy(2 * step + 2, idx_a, sems.at[0]).wait()
        for d in range(depth):
            compute(2 * half - depth + d, tiles[d])
            gather(idx_a, d, tiles[d])


U_DEPTH = 16
V_DEPTH = 8

def _rows8(tb, body):
    def step(r, carry):
        body(pl.multiple_of(r * SUBLANES, SUBLANES))
        return carry

    lax.fori_loop(0, tb // SUBLANES, step, 0)


def _peer_u_kernel(idx_hbm, xn_ref, gate_ref, tab_hbm, sc_ref, tab_vmem, *scratch):
    tiles, (part3_ref, part_ref, idx_a, idx_b, sem, idx_sems) = scratch[:U_DEPTH], scratch[U_DEPTH:]
    tb = xn_ref.shape[0]
    _load_table_once(tab_hbm, tab_vmem, sem)
    sub = lax.broadcasted_iota(jnp.int32, (ROW_CHUNKS, TILE_ROWS), 0)
    lane = lax.broadcasted_iota(jnp.int32, (ROW_CHUNKS, TILE_ROWS), 1)
    own_chunk = (lane % ROW_CHUNKS) == sub

    def compute(t, tile_ref):
        rows = pltpu.bitcast(tile_ref[...], BF16)
        xw = xn_ref[t].astype(BF16)
        r = _dot_nt(xw, rows)
        part3_ref[t] = jnp.where(own_chunk, r, 0.0)

    _token_pipeline(idx_hbm, idx_a, idx_b, idx_sems, tab_vmem, tiles, compute)

    def add_sublanes(r8):
        acc = part3_ref[pl.ds(r8, SUBLANES), 0, :]
        for s in range(1, ROW_CHUNKS):
            acc = acc + part3_ref[pl.ds(r8, SUBLANES), s, :]
        part_ref[pl.ds(r8, SUBLANES), :] = acc

    _rows8(tb, add_sublanes)
    act = _dot_exact01(part_ref[...], _seg_matrix(TILE_ROWS, ROW_CHUNKS, N_ACTIVE))
    sc = jax.nn.gelu(act) * gate_ref[...]
    for s in range(SUBLANES):
        sc_ref[:, s, :] = sc


def _peer_v_kernel(idx_hbm, sc_ref, hres_ref, tab_hbm, y_ref, tab_vmem, *scratch):
    tiles, (out3_ref, idx_a, idx_b, sem, idx_sems) = scratch[:V_DEPTH], scratch[V_DEPTH:]
    _load_table_once(tab_hbm, tab_vmem, sem)
    lane = lax.broadcasted_iota(jnp.int32, (SUBLANES, LANES), 1)
    sub = lax.broadcasted_iota(jnp.int32, (SUBLANES, LANES), 0)
    own_chunk = lane % ROW_CHUNKS == sub
    pairs_per_tile = LANES // ROW_CHUNKS

    def compute(t, tile_ref):
        rows = pltpu.bitcast(tile_ref[...], BF16)
        sc8 = sc_ref[t]
        smat = jnp.concatenate(
            [jnp.where(own_chunk, jnp.take_along_axis(sc8, c * pairs_per_tile + lane // ROW_CHUNKS, axis=1), 0.0)
             for c in range(TILE_ROWS // LANES)], axis=1)
        out3_ref[t] = _dot(smat.astype(BF16), rows)

    _token_pipeline(idx_hbm, idx_a, idx_b, idx_sems, tab_vmem, tiles, compute)
    for s, c in enumerate(CHUNK_ORDER):
        cols = slice(c * LANES, (c + 1) * LANES)
        y_ref[:, cols] = hres_ref[:, cols] + out3_ref[:, s, :]


def _peer_experts(idx4, gate, xn3, hres, utab, vtab, tb=2 * LANES):
    n = idx4.shape[1]
    half = tb // 2
    assert n % tb == 0 and half % LANES == 0 and half % U_DEPTH == 0 and half % V_DEPTH == 0
    row = pl.BlockSpec((tb, N_ACTIVE), lambda i: (i, 0))
    wide = pl.BlockSpec((tb, D_MODEL), lambda i: (i, 0))
    row3 = pl.BlockSpec((tb, ROW_CHUNKS, LANES), lambda i: (i, 0, 0))
    anyspace = pl.BlockSpec(memory_space=pl.ANY)
    table_and_tiles = lambda depth: ([pltpu.VMEM((N_EXPERTS * PACK_ROWS, LANES), jnp.int32)]
                                     + [pltpu.VMEM((TILE_WORD_ROWS, LANES), jnp.int32)] * depth)
    per_token = [pltpu.VMEM((tb, ROW_CHUNKS, TILE_ROWS), F32), pltpu.VMEM((tb, TILE_ROWS), F32)]
    idx_scratch = [pltpu.SMEM((N_ACTIVE, half), jnp.int32), pltpu.SMEM((N_ACTIVE, half), jnp.int32),
                   pltpu.SemaphoreType.DMA, pltpu.SemaphoreType.DMA((2,))]
    sc = pl.pallas_call(
        _peer_u_kernel,
        grid=(n // tb,),
        in_specs=[anyspace, row3, row, anyspace],
        out_specs=row3,
        out_shape=jax.ShapeDtypeStruct((n, SUBLANES, N_ACTIVE), F32),
        scratch_shapes=table_and_tiles(U_DEPTH) + per_token + idx_scratch,
        compiler_params=_cparams(("arbitrary",), VMEM_LIMIT),
        name="peer_u",
    )(idx4, xn3, gate, utab)
    return pl.pallas_call(
        _peer_v_kernel,
        grid=(n // tb,),
        in_specs=[anyspace, row3, wide, anyspace],
        out_specs=wide,
        out_shape=jax.ShapeDtypeStruct((n, D_MODEL), F32),
        scratch_shapes=table_and_tiles(V_DEPTH) + [pltpu.VMEM((tb, ROW_CHUNKS, LANES), F32)] + idx_scratch,
        compiler_params=_cparams(("arbitrary",), VMEM_LIMIT),
        name="peer_v",
    )(idx4, sc, hres, vtab)


def kernel(x_prompt, x_sample, cache_conv, state_lru, cache_k, cache_v, norm1_g, w_in, conv_w, conv_b,
           rg_w_a, rg_b_a, rg_w_x, rg_b_x, rg_lambda, q_norm_g, k_norm_g, attn_sinks, w_branch_lru,
           w_branch_attn, w_out, norm2_g, peer_w_query, peer_sub_keys, expert_u, expert_v):
    assert norm1_g.shape[0] == 1, "one layer"
    batch, seq, _ = x_prompt.shape
    dbatch, dseq, _ = x_sample.shape
    assert dseq == SUBLANES and seq % 256 == 0
    l = 0
    w_in_b = w_in[l].astype(BF16)
    lw = _lru_weights(conv_w[l], conv_b[l], rg_w_a[l], rg_b_a[l], rg_w_x[l], rg_b_x[l], rg_lambda[l])
    wl, wa, wo = (w[l].astype(BF16) for w in (w_branch_lru, w_branch_attn, w_out))
    wq = peer_w_query[l].astype(BF16)
    keys = peer_sub_keys[l].reshape(PEER_HEADS * 2, N_KEYS, KEY_DIM).astype(BF16)
    utab, vtab = _pack_table(expert_u[l]), _pack_table(expert_v[l])

    def tokens_after_mixers(x, lru_out, attn, sga, sgb):
        hres, xn2, xn3 = _post(x, lru_out, attn, sga, sgb, wl, wa, wo, norm2_g[l])
        idx4, gate = _route(xn2, wq, keys)
        return _peer_experts(idx4, gate, xn3, hres, utab, vtab)

    xp = x_prompt.reshape(batch * seq, D_MODEL)
    xr, ggr, q, k, v, sga, sgb = _inproj(xp, norm1_g[l], w_in_b, q_norm_g[l], k_norm_g[l])
    lru_p, hlast_p = _lru_prompt(xr, ggr, lw, batch, seq)
    attn_p = _attn_prompt(q, k, v, attn_sinks[l], batch, seq)
    y_p = tokens_after_mixers(xp, lru_p, attn_p, sga, sgb).reshape(batch, seq, D_MODEL)
    conv_p = xr.reshape(batch, seq, LRU_WIDTH)[:, seq - (CONV_WIDTH - 1):]
    kv5 = (batch, WINDOW, N_KV_HEADS, HEAD_DIM)
    k_p = k.reshape(batch, seq, KV_WIDTH)[:, seq - WINDOW:].reshape(kv5)
    v_p = v.reshape(batch, seq, KV_WIDTH)[:, seq - WINDOW:].reshape(kv5)

    xs = x_sample.reshape(dbatch * dseq, D_MODEL)
    xr, ggr, q, k, v, sga, sgb = _inproj(xs, norm1_g[l], w_in_b, q_norm_g[l], k_norm_g[l])
    lru_s, hlast_s = _lru_sample(xr, ggr, cache_conv[l], state_lru[l], lw)
    attn_s, k_s, v_s = _attn_sample(q, k, v, cache_k[l], cache_v[l], attn_sinks[l])
    y_s = tokens_after_mixers(xs, lru_s, attn_s, sga, sgb).reshape(dbatch, dseq, D_MODEL)
    conv_s = jnp.concatenate([cache_conv[l], xr.reshape(dbatch, dseq, LRU_WIDTH)],
                             axis=1)[:, -(CONV_WIDTH - 1):]

    st = lambda a: a[None]
    return (y_p, y_s, st(conv_p), st(hlast_p), st(k_p), st(v_p), st(conv_s), st(hlast_s), st(k_s), st(v_s))
```

```python
import functools

import jax
import jax.numpy as jnp
from jax import lax
from jax.experimental import pallas as pl
from jax.experimental.pallas import tpu as pltpu

D_MODEL = 1024
LRU_WIDTH = 1024
LRU_BLOCKS = 16
LRU_BLOCK = LRU_WIDTH // LRU_BLOCKS
CONV_WIDTH = 4
LRU_C = 8.0
N_HEADS = 16
N_KV_HEADS = 4
HEAD_DIM = 64
GROUP = N_HEADS // N_KV_HEADS
Q_WIDTH = N_HEADS * HEAD_DIM
KV_WIDTH = N_KV_HEADS * HEAD_DIM
WINDOW = 128
ATTN_BLOCK = 128
PAST_LEN = 16384
PEER_HEADS = 8
N_KEYS = 128
N_EXPERTS = N_KEYS * N_KEYS
KEY_DIM = 128
PEER_TOPK = 16
N_ACTIVE = PEER_HEADS * PEER_TOPK
IN_SPLITS = (LRU_WIDTH, LRU_WIDTH, Q_WIDTH, KV_WIDTH, KV_WIDTH, D_MODEL, D_MODEL)
IN_COLS = sum(IN_SPLITS)
EPS = 1e-6
NEG_INF = -1e30

LANES = 128
SUBLANES = 8
ROW_CHUNKS = D_MODEL // LANES
VMEM_LIMIT = 56 * 1024 * 1024

F32 = jnp.float32
BF16 = jnp.bfloat16


def _cparams(sem, vmem=None):
    return pltpu.CompilerParams(dimension_semantics=sem, vmem_limit_bytes=vmem)


def _dot(a, b):
    return jnp.dot(a, b, preferred_element_type=F32)


def _dot_nt(a, b):
    return lax.dot_general(a, b, (((1,), (1,)), ((), ())), preferred_element_type=F32)


def _dot_exact01(x, m01):
    hi = x.astype(BF16)
    r1 = x - hi.astype(F32)
    mid = r1.astype(BF16)
    lo = (r1 - mid.astype(F32)).astype(BF16)
    return _dot(hi, m01) + _dot(mid, m01) + _dot(lo, m01)


def _seg_matrix(width, seg, cols):
    c = lax.broadcasted_iota(jnp.int32, (width, cols), 0)
    h = lax.broadcasted_iota(jnp.int32, (width, cols), 1)
    return jnp.where(c // seg == h, 1.0, 0.0).astype(BF16)


def _seg_matrix_t(cols, width, seg):
    h = lax.broadcasted_iota(jnp.int32, (cols, width), 0)
    c = lax.broadcasted_iota(jnp.int32, (cols, width), 1)
    return jnp.where(c // seg == h, 1.0, 0.0).astype(BF16)


def _head_rmsnorm(t, gain_row, width):
    seg = _seg_matrix(width, HEAD_DIM, LANES)
    seg_t = _seg_matrix_t(LANES, width, HEAD_DIM)
    ssq = _dot_exact01(t * t, seg)
    inv = lax.rsqrt(ssq * (1.0 / HEAD_DIM) + EPS)
    inv_b = _dot_exact01(inv, seg_t)
    return t * inv_b * gain_row


def _inproj_kernel(x_ref, g1_ref, w_ref, qg_ref, kg_ref,
                   xr_ref, ggr_ref, q_ref, k_ref, v_ref, sga_ref, sgb_ref):
    x = x_ref[...]
    y = x * lax.rsqrt(jnp.mean(x * x, axis=-1, keepdims=True) + EPS)
    xn = (y * g1_ref[...]).astype(BF16)
    o = 0
    xr_ref[...] = _dot(xn, w_ref[:, o:o + LRU_WIDTH]); o += LRU_WIDTH
    ggr_ref[...] = jax.nn.gelu(_dot(xn, w_ref[:, o:o + LRU_WIDTH])); o += LRU_WIDTH
    q = _dot(xn, w_ref[:, o:o + Q_WIDTH]); o += Q_WIDTH
    q_ref[...] = (_head_rmsnorm(q, qg_ref[...], Q_WIDTH) * (HEAD_DIM ** -0.5)).astype(BF16)
    k = _dot(xn, w_ref[:, o:o + KV_WIDTH]); o += KV_WIDTH
    k_ref[...] = _head_rmsnorm(k, kg_ref[...], KV_WIDTH)
    v_ref[...] = _dot(xn, w_ref[:, o:o + KV_WIDTH]); o += KV_WIDTH
    sga_ref[...] = jax.nn.sigmoid(_dot(xn, w_ref[:, o:o + D_MODEL])); o += D_MODEL
    sgb_ref[...] = jax.nn.sigmoid(_dot(xn, w_ref[:, o:o + D_MODEL]))


def _inproj(x, norm1_g, w_in_bf16, q_norm_g, k_norm_g, tm=256):
    n = x.shape[0]
    assert n % tm == 0
    row = lambda w: pl.BlockSpec((tm, w), lambda i: (i, 0))
    full = lambda a: pl.BlockSpec(a.shape, lambda i: (0,) * a.ndim)
    g1 = norm1_g.reshape(1, D_MODEL)
    qg = jnp.tile(q_norm_g, N_HEADS).reshape(1, Q_WIDTH)
    kg = jnp.tile(k_norm_g, N_KV_HEADS).reshape(1, KV_WIDTH)
    outs = [(LRU_WIDTH, F32), (LRU_WIDTH, F32), (Q_WIDTH, BF16), (KV_WIDTH, F32), (KV_WIDTH, F32),
            (D_MODEL, F32), (D_MODEL, F32)]
    return pl.pallas_call(
        _inproj_kernel,
        grid=(n // tm,),
        in_specs=[row(D_MODEL), full(g1), full(w_in_bf16), full(qg), full(kg)],
        out_specs=[row(w) for w, _ in outs],
        out_shape=[jax.ShapeDtypeStruct((n, w), dt) for w, dt in outs],
        compiler_params=_cparams(("parallel",), VMEM_LIMIT),
        name="inproj",
    )(x, g1, w_in_bf16, qg, kg)


def _log_sigmoid(x):
    return jnp.minimum(x, 0.0) - jnp.log1p(jnp.exp(-jnp.abs(x)))


def _lru_gates(xc, wg_ref, ba_ref, bx_ref, lam_ref):
    g = _dot(xc.astype(BF16), wg_ref[...])
    r = jax.nn.sigmoid(g[:, :LRU_WIDTH] + ba_ref[...])
    i = jax.nn.sigmoid(g[:, LRU_WIDTH:] + bx_ref[...])
    log_a = LRU_C * r * _log_sigmoid(lam_ref[...])
    a = jnp.exp(log_a)
    b = jnp.sqrt(-jnp.tanh(log_a) * (a * a + 1.0)) * (i * xc)
    return a, b


def _shift_rows(x, s, tpos, fill):
    return jnp.where(tpos < s, fill, pltpu.roll(x, s, axis=0))


def _group_scan(a, b):
    rows, width = a.shape
    a = a.reshape(rows // SUBLANES, SUBLANES, width)
    b = b.reshape(rows // SUBLANES, SUBLANES, width)
    tpos = lax.broadcasted_iota(jnp.int32, a.shape, 1)
    s = 1
    while s < SUBLANES:
        a_prev = jnp.where(tpos < s, 1.0, pltpu.roll(a, s, axis=1))
        b_prev = jnp.where(tpos < s, 0.0, pltpu.roll(b, s, axis=1))
        b = a * b_prev + b
        a = a * a_prev
        s *= 2
    return a.reshape(rows, width), b.reshape(rows, width)


def _lru_prompt_kernel(xr_ref, ggr_ref, cw_ref, cb_ref, wg_ref, ba_ref, bx_ref, lam_ref,
                       out_ref, hlast_ref, prev_ref, h_ref):
    tc = xr_ref.shape[0]

    @pl.when(pl.program_id(1) == 0)
    def _():
        prev_ref[...] = jnp.zeros_like(prev_ref)
        h_ref[...] = jnp.zeros_like(h_ref)

    xr = xr_ref[...]
    tpos8 = lax.broadcasted_iota(jnp.int32, (SUBLANES, LRU_WIDTH), 0)
    prev = prev_ref[...]
    xc = xr * cw_ref[CONV_WIDTH - 1:CONV_WIDTH, :] + cb_ref[...]
    for s in range(1, CONV_WIDTH):
        rolled = pltpu.roll(xr, s, axis=0)
        head = jnp.where(tpos8 < s, pltpu.roll(prev, s, axis=0), rolled[:SUBLANES])
        shifted = jnp.concatenate([head, rolled[SUBLANES:]], axis=0)
        xc = xc + shifted * cw_ref[CONV_WIDTH - 1 - s:CONV_WIDTH - s, :]
    a, b = _lru_gates(xc, wg_ref, ba_ref, bx_ref, lam_ref)
    a8, b8 = _group_scan(a, b)
    carry = h_ref[0:1, :]
    groups = []
    for g in range(tc // SUBLANES):
        rows = slice(g * SUBLANES, (g + 1) * SUBLANES)
        groups.append(a8[rows] * carry + b8[rows])
        carry = groups[-1][SUBLANES - 1:]
    h = jnp.concatenate(groups, axis=0)
    out_ref[...] = (h * ggr_ref[...]).astype(out_ref.dtype)
    last8 = groups[-1]
    hlast_ref[...] = last8
    h_ref[...] = jnp.broadcast_to(last8[SUBLANES - 1:], h_ref.shape)
    prev_ref[...] = xr[tc - SUBLANES:]


def _lru_sample_kernel(xr_ref, ggr_ref, halo_ref, h0_ref, cw_ref, cb_ref, wg_ref, ba_ref, bx_ref,
                       lam_ref, out_ref, h_out_ref):
    rows = xr_ref.shape[0]
    xr = xr_ref[...]
    tpos = lax.broadcasted_iota(jnp.int32, (rows, LRU_WIDTH), 0) % SUBLANES
    xc = xr * cw_ref[CONV_WIDTH - 1:CONV_WIDTH, :] + cb_ref[...]
    for s in range(1, CONV_WIDTH):
        shifted = _shift_rows(xr, s, tpos, halo_ref[s - 1])
        xc = xc + shifted * cw_ref[CONV_WIDTH - 1 - s:CONV_WIDTH - s, :]
    a, b = _lru_gates(xc, wg_ref, ba_ref, bx_ref, lam_ref)
    b = b + a * h0_ref[...]
    _, h = _group_scan(a, b)
    out_ref[...] = (h * ggr_ref[...]).astype(out_ref.dtype)
    h_out_ref[...] = h


def _lru_weights(conv_w, conv_b, rg_w_a, rg_b_a, rg_w_x, rg_b_x, rg_lambda):
    eye = jnp.eye(LRU_BLOCKS, dtype=F32)
    bd = lambda w: jnp.einsum("njk,nm->njmk", w, eye).reshape(LRU_WIDTH, LRU_WIDTH)
    wg = jnp.concatenate([bd(rg_w_a), bd(rg_w_x)], axis=1).astype(BF16)
    r = lambda v: v.reshape(1, LRU_WIDTH)
    return conv_w, r(conv_b), wg, r(rg_b_a), r(rg_b_x), r(rg_lambda)


def _lru_prompt(xr, ggr, lw, batch, seq, tc=256):
    n = xr.shape[0]
    nt = seq // tc
    row = pl.BlockSpec((tc, LRU_WIDTH), lambda b, t: (b * nt + t, 0))
    full = lambda a: pl.BlockSpec(a.shape, lambda b, t: (0,) * a.ndim)
    out, hlast = pl.pallas_call(
        _lru_prompt_kernel,
        grid=(batch, nt),
        in_specs=[row, row] + [full(a) for a in lw],
        out_specs=[row, pl.BlockSpec((SUBLANES, LRU_WIDTH), lambda b, t: (b, 0))],
        out_shape=[jax.ShapeDtypeStruct((n, LRU_WIDTH), BF16),
                   jax.ShapeDtypeStruct((batch * SUBLANES, LRU_WIDTH), F32)],
        scratch_shapes=[pltpu.VMEM((SUBLANES, LRU_WIDTH), F32), pltpu.VMEM((SUBLANES, LRU_WIDTH), F32)],
        compiler_params=_cparams(("parallel", "arbitrary"), VMEM_LIMIT),
        name="lru_prompt",
    )(xr, ggr, *lw)
    return out, hlast.reshape(batch, SUBLANES, LRU_WIDTH)[:, SUBLANES - 1]


def _lru_sample(xr, ggr, conv_buf, h0, lw, rows_per_step=256):
    n = xr.shape[0]
    nb = n // SUBLANES
    halos = []
    for s in range(1, CONV_WIDTH):
        rows = conv_buf[:, CONV_WIDTH - 1 - s:, :]
        halos.append(jnp.concatenate(
            [rows, jnp.zeros((nb, SUBLANES - s, LRU_WIDTH), F32)], axis=1).reshape(n, LRU_WIDTH))
    halo = jnp.stack(halos)
    h0_rows = jnp.concatenate(
        [h0[:, None, :], jnp.zeros((nb, SUBLANES - 1, LRU_WIDTH), F32)], axis=1).reshape(n, LRU_WIDTH)
    r = rows_per_step
    row = pl.BlockSpec((r, LRU_WIDTH), lambda i: (i, 0))
    full = lambda a: pl.BlockSpec(a.shape, lambda i: (0,) * a.ndim)
    out, h = pl.pallas_call(
        _lru_sample_kernel,
        grid=(n // r,),
        in_specs=[row, row, pl.BlockSpec((CONV_WIDTH - 1, r, LRU_WIDTH), lambda i: (0, i, 0)), row]
        + [full(a) for a in lw],
        out_specs=[row, row],
        out_shape=[jax.ShapeDtypeStruct((n, LRU_WIDTH), BF16), jax.ShapeDtypeStruct((n, LRU_WIDTH), F32)],
        compiler_params=_cparams(("parallel",), VMEM_LIMIT),
        name="lru_sample",
    )(xr, ggr, halo, h0_rows, *lw)
    return out, h.reshape(nb, SUBLANES, LRU_WIDTH)[:, SUBLANES - 1]


def _attend_heads(q, k, v, slopes_ref, sinks_ref, distf, valid):
    heads = range(N_HEADS)
    col = lambda a, i: a[:, i * HEAD_DIM:(i + 1) * HEAD_DIM]
    kv_head = lambda a, g: a[g] if isinstance(a, (list, tuple)) else col(a, g)
    ss = [jnp.where(valid, _dot_nt(col(q, h), kv_head(k, h // GROUP)) - slopes_ref[h] * distf, NEG_INF)
          for h in heads]
    ms = [jnp.maximum(jnp.max(ss[h], axis=-1, keepdims=True), sinks_ref[h]) for h in heads]
    ps = [jnp.exp(ss[h] - ms[h]) for h in heads]
    dens = [jnp.sum(ps[h], axis=-1, keepdims=True) + jnp.exp(sinks_ref[h] - ms[h]) for h in heads]
    outs = [_dot(ps[h].astype(BF16), kv_head(v, h // GROUP)) / dens[h] for h in heads]
    return jnp.concatenate(outs, axis=-1)


def _attn_prompt_kernel(slopes_ref, sinks_ref, q_ref, kp_ref, kc_ref, vp_ref, vc_ref, o_ref):
    blk = pl.program_id(1)
    tq = ATTN_BLOCK
    qi = lax.broadcasted_iota(jnp.int32, (tq, 2 * tq), 0)
    kj = lax.broadcasted_iota(jnp.int32, (tq, 2 * tq), 1)
    dist = (tq + qi) - kj
    valid = (dist >= 0) & (dist <= WINDOW) & ((kj >= tq) | (blk > 0))
    distf = dist.astype(F32)
    k = jnp.concatenate([kp_ref[...], kc_ref[...]], axis=0).astype(BF16)
    v = jnp.concatenate([vp_ref[...], vc_ref[...]], axis=0).astype(BF16)
    o_ref[...] = _attend_heads(q_ref[...], k, v, slopes_ref, sinks_ref, distf, valid).astype(o_ref.dtype)


def _alibi_slopes():
    return 2.0 ** (-8.0 * jnp.arange(1, N_HEADS + 1, dtype=F32) / N_HEADS)


def _attn_prompt(q, k, v, sinks, batch, seq):
    n = q.shape[0]
    nb = seq // ATTN_BLOCK
    smem = pl.BlockSpec(memory_space=pltpu.SMEM)
    cur = lambda w: pl.BlockSpec((ATTN_BLOCK, w), lambda b, i: (b * nb + i, 0))
    prv = lambda w: pl.BlockSpec((ATTN_BLOCK, w), lambda b, i: (b * nb + jnp.maximum(i - 1, 0), 0))
    return pl.pallas_call(
        _attn_prompt_kernel,
        grid=(batch, nb),
        in_specs=[smem, smem, cur(Q_WIDTH), prv(KV_WIDTH), cur(KV_WIDTH), prv(KV_WIDTH), cur(KV_WIDTH)],
        out_specs=cur(Q_WIDTH),
        out_shape=jax.ShapeDtypeStruct((n, Q_WIDTH), BF16),
        compiler_params=_cparams(("parallel", "arbitrary"), VMEM_LIMIT),
        name="attn_prompt",
    )(_alibi_slopes(), sinks.astype(F32), q, k, k, v, v)


def _attn_sample_kernel(slopes_ref, sinks_ref, q_ref, kn_ref, vn_ref, ck_ref, cv_ref,
                        o_ref, ko_ref, vo_ref):
    nseq = ck_ref.shape[0]
    t = SUBLANES
    keys = WINDOW + t
    qi = lax.broadcasted_iota(jnp.int32, (t, keys), 0)
    kj = lax.broadcasted_iota(jnp.int32, (t, keys), 1)
    dist = (WINDOW + qi) - kj
    valid = (dist >= 0) & (dist <= WINDOW)
    distf = dist.astype(F32)

    def one_seq(b, carry):
        kn = kn_ref[b]
        vn = vn_ref[b]
        kcols, vcols = [], []
        for g in range(N_KV_HEADS):
            cols = slice(g * HEAD_DIM, (g + 1) * HEAD_DIM)
            kall = jnp.concatenate([ck_ref[b, :, g, :], kn[:, cols]], axis=0)
            vall = jnp.concatenate([cv_ref[b, :, g, :], vn[:, cols]], axis=0)
            ko_ref[b, :, g, :] = kall[t:]
            vo_ref[b, :, g, :] = vall[t:]
            kcols.append(kall.astype(BF16))
            vcols.append(vall.astype(BF16))
        o_ref[b] = _attend_heads(q_ref[b], kcols, vcols, slopes_ref, sinks_ref, distf, valid).astype(o_ref.dtype)
        return carry

    lax.fori_loop(0, nseq, one_seq, 0)


def _attn_sample(q, k_new, v_new, cache_k, cache_v, sinks, seqs_per_step=8):
    nb = cache_k.shape[0]
    t = SUBLANES
    sb = seqs_per_step
    smem = pl.BlockSpec(memory_space=pltpu.SMEM)
    blk = lambda r, w: pl.BlockSpec((sb, r, w), lambda i: (i, 0, 0))
    cache = pl.BlockSpec((sb, WINDOW, N_KV_HEADS, HEAD_DIM), lambda i: (i, 0, 0, 0))
    cache_shape = jax.ShapeDtypeStruct((nb, WINDOW, N_KV_HEADS, HEAD_DIM), F32)
    o, ko, vo = pl.pallas_call(
        _attn_sample_kernel,
        grid=(nb // sb,),
        in_specs=[smem, smem, blk(t, Q_WIDTH), blk(t, KV_WIDTH), blk(t, KV_WIDTH), cache, cache],
        out_specs=[blk(t, Q_WIDTH), cache, cache],
        out_shape=[jax.ShapeDtypeStruct((nb, t, Q_WIDTH), BF16), cache_shape, cache_shape],
        compiler_params=_cparams(("parallel",), VMEM_LIMIT),
        name="attn_sample",
    )(_alibi_slopes(), sinks.astype(F32), q.reshape(nb, t, Q_WIDTH), k_new.reshape(nb, t, KV_WIDTH),
      v_new.reshape(nb, t, KV_WIDTH), cache_k, cache_v)
    return o.reshape(nb * t, Q_WIDTH), ko, vo


def _post_kernel(x_ref, lru_ref, attn_ref, sga_ref, sgb_ref, wl_ref, wa_ref, wo_ref, g2_ref,
                 hres_ref, xn2_ref, xn3_ref):
    merged = sga_ref[...] * _dot(lru_ref[...], wl_ref[...]) + sgb_ref[...] * _dot(attn_ref[...], wa_ref[...])
    hres = x_ref[...] + _dot(merged.astype(BF16), wo_ref[...])
    hres_ref[...] = hres
    y = hres * lax.rsqrt(jnp.mean(hres * hres, axis=-1, keepdims=True) + EPS)
    xn2 = y * g2_ref[...]
    xn2_ref[...] = xn2.astype(BF16)
    for s, c in enumerate(CHUNK_ORDER):
        xn3_ref[:, s, :] = xn2[:, c * LANES:(c + 1) * LANES]


def _post(x, lru_out, attn, sga, sgb, wl, wa, wo, norm2_g, tm=256):
    n = x.shape[0]
    row = pl.BlockSpec((tm, D_MODEL), lambda i: (i, 0))
    row3 = pl.BlockSpec((tm, ROW_CHUNKS, LANES), lambda i: (i, 0, 0))
    full = lambda a: pl.BlockSpec(a.shape, lambda i: (0,) * a.ndim)
    g2 = norm2_g.reshape(1, D_MODEL)
    return pl.pallas_call(
        _post_kernel,
        grid=(n // tm,),
        in_specs=[row] * 5 + [full(wl), full(wa), full(wo), full(g2)],
        out_specs=[row, row, row3],
        out_shape=[jax.ShapeDtypeStruct((n, D_MODEL), F32), jax.ShapeDtypeStruct((n, D_MODEL), BF16),
                   jax.ShapeDtypeStruct((n, ROW_CHUNKS, LANES), F32)],
        compiler_params=_cparams(("parallel",), VMEM_LIMIT),
        name="post",
    )(x, lru_out, attn, sga, sgb, wl, wa, wo, g2)


def _topk_rows(s, k, pos):
    vals, sels = [], []
    for _ in range(k):
        m = jnp.max(s, axis=0, keepdims=True)
        sel = jnp.min(jnp.where(s == m, pos, float(1 << 20)), axis=0, keepdims=True)
        vals.append(m)
        sels.append(sel)
        s = jnp.where(pos == sel, -jnp.inf, s)
    return jnp.concatenate(vals, axis=0), jnp.concatenate(sels, axis=0)


def _pick_rows(table, which):
    out = jnp.zeros(which.shape, table.dtype)
    for a in range(table.shape[0]):
        out = out + jnp.where(which == float(a), table[a:a + 1, :], 0.0)
    return out


def _pair_candidates(s1, s2):
    tokens = s1.shape[1]
    row = lax.broadcasted_iota(jnp.int32, (SUBLANES, tokens), 0)
    rowf = row.astype(F32)
    neg = -jnp.inf
    lo1, hi1 = s1[:SUBLANES], s1[SUBLANES:]
    lo2, hi2 = s2[:SUBLANES], s2[SUBLANES:]
    k = float(PEER_TOPK)
    pieces = [
        (s1[0:1] + lo2, rowf),
        (s1[0:1] + hi2, rowf + 8.0),
        (s1[1:2] + lo2, rowf + k),
        (jnp.where(row < 5, s1[2:3] + lo2, neg), rowf + 2 * k),
        (jnp.where(row < 4, s1[3:4] + lo2, neg), rowf + 3 * k),
        (jnp.where(row >= 4, lo1 + s2[0:1], neg), rowf * k),
        (hi1 + s2[0:1], (rowf + 8.0) * k),
        (jnp.where(row >= 4, lo1 + s2[1:2], neg), rowf * k + 1.0),
        (jnp.where(row == 4, lo1 + s2[2:3], neg), rowf * k + 2.0),
    ]
    return (jnp.concatenate([p[0] for p in pieces], axis=0),
            jnp.concatenate([p[1] for p in pieces], axis=0))


def _route_kernel(xn_ref, wq_ref, keys_ref, row_ref, gate_ref):
    qr = _dot(xn_ref[...], wq_ref[...]).astype(BF16)
    tokens = qr.shape[0]
    key_pos = lax.broadcasted_iota(jnp.int32, (N_KEYS, tokens), 0).astype(F32)
    rows_of, gates = [], []
    for h in range(PEER_HEADS):
        tops = []
        for p in range(2):
            c = h * 2 + p
            st = _dot_nt(keys_ref[c], qr[:, c * KEY_DIM:(c + 1) * KEY_DIM])
            tops.append(_topk_rows(st, PEER_TOPK, key_pos))
        (s1, i1), (s2, i2) = tops
        cand, cand_pos = _pair_candidates(s1, s2)
        best, pos = _topk_rows(cand, PEER_TOPK, cand_pos)
        rank1 = jnp.floor(pos * (1.0 / PEER_TOPK))
        rank2 = pos - rank1 * PEER_TOPK
        expert = _pick_rows(i1, rank1) * N_KEYS + _pick_rows(i2, rank2)
        rows_of.append(expert * float(PACK_ROWS))
        ex = jnp.exp(best - best[0:1, :])
        gates.append(ex / jnp.sum(ex, axis=0, keepdims=True))
    row_ref[...] = jnp.concatenate(rows_of, axis=0).astype(jnp.int32)
    gate_ref[...] = jnp.concatenate(gates, axis=0).T


def _route(xn2, wq_bf16, keys_bf16, tm=128):
    n = xn2.shape[0]
    full = lambda a: pl.BlockSpec(a.shape, lambda i: (0,) * a.ndim)
    col = pl.BlockSpec((tm, N_ACTIVE), lambda i: (i, 0))
    return pl.pallas_call(
        _route_kernel,
        grid=(n // tm,),
        in_specs=[pl.BlockSpec((tm, D_MODEL), lambda i: (i, 0)), full(wq_bf16), full(keys_bf16)],
        out_specs=[pl.BlockSpec((N_ACTIVE, tm), lambda i: (0, i)), col],
        out_shape=[jax.ShapeDtypeStruct((N_ACTIVE, n), jnp.int32), jax.ShapeDtypeStruct((n, N_ACTIVE), F32)],
        compiler_params=_cparams(("parallel",), VMEM_LIMIT),
        name="route",
    )(xn2, wq_bf16, keys_bf16)


PACK_ROWS = ROW_CHUNKS // 2
TILE_WORD_ROWS = N_ACTIVE * PACK_ROWS
TILE_ROWS = N_ACTIVE * ROW_CHUNKS


CHUNK_ORDER = tuple(c for a in range(PACK_ROWS) for c in (a, a + PACK_ROWS))


def _pack_table(w):
    wb = w.astype(BF16)
    half = D_MODEL // 2
    pairs = jnp.stack([wb[:, :half], wb[:, half:]], axis=-1)
    return lax.bitcast_convert_type(pairs, jnp.int32).reshape(N_EXPERTS * PACK_ROWS, LANES)


def _load_table_once(tab_hbm, tab_vmem, sem):
    @pl.when(pl.program_id(0) == 0)
    def _():
        cp = pltpu.make_async_copy(tab_hbm, tab_vmem, sem)
        cp.start()
        cp.wait()


def _token_pipeline(idx_hbm, idx_a, idx_b, sems, tab_vmem, tiles, compute):
    step, nsteps = pl.program_id(0), pl.num_programs(0)
    half = idx_a.shape[1]
    depth = len(tiles)

    def idx_copy(block, dst, sem):
        return pltpu.make_async_copy(idx_hbm.at[:, pl.ds(block * half, half)], dst, sem)

    def gather(idx_smem, t, tile_ref):
        for j in range(N_ACTIVE):
            e4 = pl.multiple_of(idx_smem.at[j][t], PACK_ROWS)
            tile_ref[j * PACK_ROWS:(j + 1) * PACK_ROWS, :] = tab_vmem[pl.ds(e4, PACK_ROWS), :]

    def token_groups(idx_smem, base):
        def body(i, carry):
            t = depth * i
            for d in range(depth):
                compute(base + t + d, tiles[d])
                gather(idx_smem, t + depth + d, tiles[d])
            return carry

        lax.fori_loop(0, half // depth - 1, body, 0)

    last = step + 1 == nsteps

    @pl.when(step == 0)
    def _():
        idx_copy(0, idx_a, sems.at[0]).start()
        idx_copy(1, idx_b, sems.at[1]).start()
        idx_copy(0, idx_a, sems.at[0]).wait()
        for d in range(depth):
            gather(idx_a, d, tiles[d])

    token_groups(idx_a, 0)

    @pl.when(jnp.logical_not(last))
    def _():
        idx_copy(2 * step + 2, idx_a, sems.at[0]).start()

    idx_copy(2 * step + 1, idx_b, sems.at[1]).wait()
    for d in range(depth):
        compute(half - depth + d, tiles[d])
        gather(idx_b, d, tiles[d])
    token_groups(idx_b, half)

    @pl.when(last)
    def _():
        for d in range(depth):
            compute(2 * half - depth + d, tiles[d])

    @pl.when(jnp.logical_not(last))
    def _():
        idx_copy(2 * step + 3, idx_b, sems.at[1]).start()
        idx_copy(2 * step + 2, idx_a, sems.at[0]).wait()
        for d in range(depth):
            compute(2 * half - depth + d, tiles[d])
            gather(idx_a, d, tiles[d])


U_DEPTH = 16
V_DEPTH = 8

def _rows8(tb, body):
    def step(r, carry):
        body(pl.multiple_of(r * SUBLANES, SUBLANES))
        return carry

    lax.fori_loop(0, tb // SUBLANES, step, 0)


def _peer_u_kernel(idx_hbm, xn_ref, gate_ref, tab_hbm, sc_ref, tab_vmem, *scratch):
    tiles, (part3_ref, part_ref, idx_a, idx_b, sem, idx_sems) = scratch[:U_DEPTH], scratch[U_DEPTH:]
    tb = xn_ref.shape[0]
    _load_table_once(tab_hbm, tab_vmem, sem)
    sub = lax.broadcasted_iota(jnp.int32, (ROW_CHUNKS, TILE_ROWS), 0)
    lane = lax.broadcasted_iota(jnp.int32, (ROW_CHUNKS, TILE_ROWS), 1)
    own_chunk = (lane % ROW_CHUNKS) == sub

    def compute(t, tile_ref):
        rows = pltpu.bitcast(tile_ref[...], BF16)
        xw = xn_ref[t].astype(BF16)
        r = _dot_nt(xw, rows)
        part3_ref[t] = jnp.where(own_chunk, r, 0.0)

    _token_pipeline(idx_hbm, idx_a, idx_b, idx_sems, tab_vmem, tiles, compute)

    def add_sublanes(r8):
        acc = part3_ref[pl.ds(r8, SUBLANES), 0, :]
        for s in range(1, ROW_CHUNKS):
            acc = acc + part3_ref[pl.ds(r8, SUBLANES), s, :]
        part_ref[pl.ds(r8, SUBLANES), :] = acc

    _rows8(tb, add_sublanes)
    act = _dot_exact01(part_ref[...], _seg_matrix(TILE_ROWS, ROW_CHUNKS, N_ACTIVE))
    sc = jax.nn.gelu(act) * gate_ref[...]
    for s in range(SUBLANES):
        sc_ref[:, s, :] = sc


def _peer_v_kernel(idx_hbm, sc_ref, hres_ref, tab_hbm, y_ref, tab_vmem, *scratch):
    tiles, (out3_ref, idx_a, idx_b, sem, idx_sems) = scratch[:V_DEPTH], scratch[V_DEPTH:]
    _load_table_once(tab_hbm, tab_vmem, sem)
    lane = lax.broadcasted_iota(jnp.int32, (SUBLANES, LANES), 1)
    sub = lax.broadcasted_iota(jnp.int32, (SUBLANES, LANES), 0)
    own_chunk = lane % ROW_CHUNKS == sub
    pairs_per_tile = LANES // ROW_CHUNKS

    def compute(t, tile_ref):
        rows = pltpu.bitcast(tile_ref[...], BF16)
        sc8 = sc_ref[t]
        smat = jnp.concatenate(
            [jnp.where(own_chunk, jnp.take_along_axis(sc8, c * pairs_per_tile + lane // ROW_CHUNKS, axis=1), 0.0)
             for c in range(TILE_ROWS // LANES)], axis=1)
        out3_ref[t] = _dot(smat.astype(BF16), rows)

    _token_pipeline(idx_hbm, idx_a, idx_b, idx_sems, tab_vmem, tiles, compute)
    for s, c in enumerate(CHUNK_ORDER):
        cols = slice(c * LANES, (c + 1) * LANES)
        y_ref[:, cols] = hres_ref[:, cols] + out3_ref[:, s, :]


def _peer_experts(idx4, gate, xn3, hres, utab, vtab, tb=2 * LANES):
    n = idx4.shape[1]
    half = tb // 2
    assert n % tb == 0 and half % LANES == 0 and half % U_DEPTH == 0 and half % V_DEPTH == 0
    row = pl.BlockSpec((tb, N_ACTIVE), lambda i: (i, 0))
    wide = pl.BlockSpec((tb, D_MODEL), lambda i: (i, 0))
    row3 = pl.BlockSpec((tb, ROW_CHUNKS, LANES), lambda i: (i, 0, 0))
    anyspace = pl.BlockSpec(memory_space=pl.ANY)
    table_and_tiles = lambda depth: ([pltpu.VMEM((N_EXPERTS * PACK_ROWS, LANES), jnp.int32)]
                                     + [pltpu.VMEM((TILE_WORD_ROWS, LANES), jnp.int32)] * depth)
    per_token = [pltpu.VMEM((tb, ROW_CHUNKS, TILE_ROWS), F32), pltpu.VMEM((tb, TILE_ROWS), F32)]
    idx_scratch = [pltpu.SMEM((N_ACTIVE, half), jnp.int32), pltpu.SMEM((N_ACTIVE, half), jnp.int32),
                   pltpu.SemaphoreType.DMA, pltpu.SemaphoreType.DMA((2,))]
    sc = pl.pallas_call(
        _peer_u_kernel,
        grid=(n // tb,),
        in_specs=[anyspace, row3, row, anyspace],
        out_specs=row3,
        out_shape=jax.ShapeDtypeStruct((n, SUBLANES, N_ACTIVE), F32),
        scratch_shapes=table_and_tiles(U_DEPTH) + per_token + idx_scratch,
        compiler_params=_cparams(("arbitrary",), VMEM_LIMIT),
        name="peer_u",
    )(idx4, xn3, gate, utab)
    return pl.pallas_call(
        _peer_v_kernel,
        grid=(n // tb,),
        in_specs=[anyspace, row3, wide, anyspace],
        out_specs=wide,
        out_shape=jax.ShapeDtypeStruct((n, D_MODEL), F32),
        scratch_shapes=table_and_tiles(V_DEPTH) + [pltpu.VMEM((tb, ROW_CHUNKS, LANES), F32)] + idx_scratch,
        compiler_params=_cparams(("arbitrary",), VMEM_LIMIT),
        name="peer_v",
    )(idx4, sc, hres, vtab)


def kernel(x_prompt, x_sample, cache_conv, state_lru, cache_k, cache_v, norm1_g, w_in, conv_w, conv_b,
           rg_w_a, rg_b_a, rg_w_x, rg_b_x, rg_lambda, q_norm_g, k_norm_g, attn_sinks, w_branch_lru,
           w_branch_attn, w_out, norm2_g, peer_w_query, peer_sub_keys, expert_u, expert_v):
    assert norm1_g.shape[0] == 1, "one layer"
    batch, seq, _ = x_prompt.shape
    dbatch, dseq, _ = x_sample.shape
    assert dseq == SUBLANES and seq % 256 == 0
    l = 0
    w_in_b = w_in[l].astype(BF16)
    lw = _lru_weights(conv_w[l], conv_b[l], rg_w_a[l], rg_b_a[l], rg_w_x[l], rg_b_x[l], rg_lambda[l])
    wl, wa, wo = (w[l].astype(BF16) for w in (w_branch_lru, w_branch_attn, w_out))
    wq = peer_w_query[l].astype(BF16)
    keys = peer_sub_keys[l].reshape(PEER_HEADS * 2, N_KEYS, KEY_DIM).astype(BF16)
    utab, vtab = _pack_table(expert_u[l]), _pack_table(expert_v[l])

    def tokens_after_mixers(x, lru_out, attn, sga, sgb):
        hres, xn2, xn3 = _post(x, lru_out, attn, sga, sgb, wl, wa, wo, norm2_g[l])
        idx4, gate = _route(xn2, wq, keys)
        return _peer_experts(idx4, gate, xn3, hres, utab, vtab)

    xp = x_prompt.reshape(batch * seq, D_MODEL)
    xr, ggr, q, k, v, sga, sgb = _inproj(xp, norm1_g[l], w_in_b, q_norm_g[l], k_norm_g[l])
    lru_p, hlast_p = _lru_prompt(xr, ggr, lw, batch, seq)
    attn_p = _attn_prompt(q, k, v, attn_sinks[l], batch, seq)
    y_p = tokens_after_mixers(xp, lru_p, attn_p, sga, sgb).reshape(batch, seq, D_MODEL)
    conv_p = xr.reshape(batch, seq, LRU_WIDTH)[:, seq - (CONV_WIDTH - 1):]
    kv5 = (batch, WINDOW, N_KV_HEADS, HEAD_DIM)
    k_p = k.reshape(batch, seq, KV_WIDTH)[:, seq - WINDOW:].reshape(kv5)
    v_p = v.reshape(batch, seq, KV_WIDTH)[:, seq - WINDOW:].reshape(kv5)

    xs = x_sample.reshape(dbatch * dseq, D_MODEL)
    xr, ggr, q, k, v, sga, sgb = _inproj(xs, norm1_g[l], w_in_b, q_norm_g[l], k_norm_g[l])
    lru_s, hlast_s = _lru_sample(xr, ggr, cache_conv[l], state_lru[l], lw)
    attn_s, k_s, v_s = _attn_sample(q, k, v, cache_k[l], cache_v[l], attn_sinks[l])
    y_s = tokens_after_mixers(xs, lru_s, attn_s, sga, sgb).reshape(dbatch, dseq, D_MODEL)
    conv_s = jnp.concatenate([cache_conv[l], xr.reshape(dbatch, dseq, LRU_WIDTH)],
                             axis=1)[:, -(CONV_WIDTH - 1):]

    st = lambda a: a[None]
    return (y_p, y_s, st(conv_p), st(hlast_p), st(k_p), st(v_p), st(conv_s), st(hlast_s), st(k_s), st(v_s))
```
